```python
import math, functools
import jax, jax.numpy as jnp
from jax import lax
import numpy as np

D_MODEL = 1024
BATCH = 8
SEQ = 2048
DEPTH = 1
DEC_BATCH = 32
DEC_SEQ = 4
PAST_LEN = 16384
PAGE_SIZE = 128

N_HEADS = 8
HEAD_DIM = 64
V_DIM = 2 * HEAD_DIM
ATT_WIDTH = N_HEADS * V_DIM
ROPE_THETA = 10000.0
Q_BLOCK = 128
EXPAND = 2
D_INNER = EXPAND * D_MODEL
SSM_HEAD_DIM = 64
N_SSM_HEADS = D_INNER // SSM_HEAD_DIM
N_GROUPS = 4
HEADS_PER_GROUP = N_SSM_HEADS // N_GROUPS
D_STATE = 128
CONV_W = 4
CONV_DIM = D_INNER + 2 * N_GROUPS * D_STATE
SSD_CHUNK = 128
D_FF = 4 * D_MODEL
EPS = 1e-6
SPLIT_SIZES = (ATT_WIDTH, ATT_WIDTH, ATT_WIDTH, D_INNER, CONV_DIM, N_SSM_HEADS, D_MODEL, D_MODEL)
D_IN_PROJ = sum(SPLIT_SIZES)

kernel_name = "hybrid_diffattn_mamba2_decode_step"

F32 = jnp.float32


def rmsnorm(x, g):
    xf = x.astype(F32)
    y = xf * lax.rsqrt(jnp.mean(xf * xf, axis=-1, keepdims=True) + EPS)
    return (y * g.astype(F32)).astype(x.dtype)


def rope(x, pos):
    half = HEAD_DIM // 2
    inv = ROPE_THETA ** (-jnp.arange(half, dtype=F32) / half)
    ang = pos.astype(F32)[:, None] * inv[None, :]
    cos = jnp.cos(ang)[None, :, None, None, :]
    sin = jnp.sin(ang)[None, :, None, None, :]
    xf = x.astype(F32)
    x1, x2 = xf[..., :half], xf[..., half:]
    return jnp.concatenate([x1 * cos - x2 * sin, x2 * cos + x1 * sin], axis=-1).astype(x.dtype)


def diff_attend(q, k, v, q_pos, k_pos, lam):
    s = jnp.einsum('bqhcd,bkhcd->bhcqk', q.astype(F32), k.astype(F32)) * (HEAD_DIM ** -0.5)
    mask = k_pos[None, :] <= q_pos[:, None]
    s = jnp.where(mask, s, -jnp.inf)
    p = jax.nn.softmax(s, axis=-1)
    a = p[:, :, 0] - lam * p[:, :, 1]
    return jnp.einsum('bhqk,bkhe->bqhe', a, v.astype(F32))


def prompt_attention(q, k, v, lam):
    b, t = q.shape[:2]
    nb = t // Q_BLOCK
    qb = q.reshape(b, nb, Q_BLOCK, N_HEADS, 2, HEAD_DIM).swapaxes(0, 1)
    pos = jnp.arange(t)
    qpos = pos.reshape(nb, Q_BLOCK)
    o = lax.map(lambda args: diff_attend(args[0], k, v, args[1], pos, lam), (qb, qpos))
    return o.swapaxes(0, 1).reshape(b, t, N_HEADS, V_DIM)


def sample_attention(q, k, v, lam, cache_k, cache_v, page_table, layer):
    t = q.shape[1]
    past = page_table.shape[1] * PAGE_SIZE
    k_pos = jnp.arange(past + t)
    q_pos = past + jnp.arange(t)

    def one(args):
        qs, ks, vs, pt = args
        kp = cache_k[layer, pt].reshape(past, N_HEADS, 2, HEAD_DIM)
        vp = cache_v[layer, pt].reshape(past, N_HEADS, V_DIM)
        kk = jnp.concatenate([kp.astype(ks.dtype), ks], axis=0)
        vv = jnp.concatenate([vp.astype(vs.dtype), vs], axis=0)
        return diff_attend(qs[None], kk[None], vv[None], q_pos, k_pos, lam)[0]

    return lax.map(one, (q, k, v, page_table))


def ssd_scan(x, dt, A, B, C, h0):
    b, t = x.shape[:2]
    l = SSD_CHUNK if t % SSD_CHUNK == 0 else t
    c = t // l
    G, R, P, N = N_GROUPS, HEADS_PER_GROUP, SSM_HEAD_DIM, D_STATE
    x = x.reshape(b, c, l, G, R, P)
    dt = dt.reshape(b, c, l, G, R)
    B = B.reshape(b, c, l, G, N)
    C = C.reshape(b, c, l, G, N)
    a = jnp.cumsum(dt * A, axis=2)
    seg = a[:, :, :, None] - a[:, :, None, :]
    causal = jnp.tril(jnp.ones((l, l), dtype=bool))[:, :, None, None]
    Lmat = jnp.exp(jnp.where(causal, seg, -jnp.inf))
    y_diag = jnp.einsum('bcign,bcjgn,bcijgr,bcjgr,bcjgrp->bcigrp', C, B, Lmat, dt, x)
    decay_end = jnp.exp(a[:, :, -1:] - a)
    chunk_states = jnp.einsum('bcjgn,bcjgr,bcjgrp->bcgrpn', B, decay_end * dt, x)
    chunk_decay = jnp.exp(a[:, :, -1])

    def step(h, inp):
        s, d = inp
        return d[..., None, None] * h + s, h

    h_final, h_in = lax.scan(step, h0, (chunk_states.swapaxes(0, 1), chunk_decay.swapaxes(0, 1)))
    h_in = h_in.swapaxes(0, 1)
    y_off = jnp.einsum('bcign,bcgrpn,bcigr->bcigrp', C, h_in, jnp.exp(a))
    return (y_diag + y_off).reshape(b, t, G, R, P), h_final


def ssm_branch(z, xbc, dt_raw, conv_state, ssm_state, p):
    b, t, _ = xbc.shape
    xpad = jnp.concatenate([conv_state.astype(xbc.dtype), xbc], axis=1)
    new_conv = xpad[:, -(CONV_W - 1):]
    conv = p['conv_b'] + xpad[:, 0:t] * p['conv_w'][0]
    for kk in range(1, CONV_W):
        conv = conv + xpad[:, kk:kk + t] * p['conv_w'][kk]
    xbc = jax.nn.silu(conv).astype(F32)
    xs, Bm, Cm = jnp.split(xbc, [D_INNER, D_INNER + N_GROUPS * D_STATE], axis=-1)
    xs = xs.reshape(b, t, N_GROUPS, HEADS_PER_GROUP, SSM_HEAD_DIM)
    Bm = Bm.reshape(b, t, N_GROUPS, D_STATE)
    Cm = Cm.reshape(b, t, N_GROUPS, D_STATE)
    dt = jax.nn.softplus(dt_raw.astype(F32) + p['dt_bias'].astype(F32)).reshape(b, t, N_GROUPS, HEADS_PER_GROUP)
    A = -jnp.exp(p['a_log'].astype(F32)).reshape(N_GROUPS, HEADS_PER_GROUP)
    h0 = ssm_state.astype(F32).reshape(b, N_GROUPS, HEADS_PER_GROUP, SSM_HEAD_DIM, D_STATE)
    y, h_final = ssd_scan(xs, dt, A, Bm, Cm, h0)
    y = y + p['d_skip'].astype(F32).reshape(N_GROUPS, HEADS_PER_GROUP)[..., None] * xs
    y = y.reshape(b, t, D_INNER) * jax.nn.silu(z.astype(F32))
    y = rmsnorm(y.reshape(b, t, N_GROUPS, D_INNER // N_GROUPS),
                p['ssm_norm_g'].reshape(N_GROUPS, D_INNER // N_GROUPS)).reshape(b, t, D_INNER)
    new_ssm = h_final.reshape(b, N_SSM_HEADS, SSM_HEAD_DIM, D_STATE).astype(ssm_state.dtype)
    return y.astype(z.dtype), new_conv, new_ssm


def mixer_layer(x, pos, attend, conv_state, ssm_state, p, lam_init):
    b, t, _ = x.shape
    h = rmsnorm(x, p['norm_mix_g'])
    proj = h @ p['w_in']
    split_idx = [int(i) for i in np.cumsum(SPLIT_SIZES)[:-1]]
    q, k, v, z, xbc, dt_raw, g_a, g_b = jnp.split(proj, split_idx, axis=-1)
    q = rope(rmsnorm(q.reshape(b, t, N_HEADS, 2, HEAD_DIM), p['q_norm_g']), pos)
    k = rope(rmsnorm(k.reshape(b, t, N_HEADS, 2, HEAD_DIM), p['k_norm_g']), pos)
    v = v.reshape(b, t, N_HEADS, V_DIM)
    lam = (jnp.exp(jnp.sum(p['lambda_q1'].astype(F32) * p['lambda_k1'].astype(F32)))
           - jnp.exp(jnp.sum(p['lambda_q2'].astype(F32) * p['lambda_k2'].astype(F32))) + lam_init)
    o = attend(q, k, v, lam)
    o = rmsnorm(o, p['subln_g']) * (1.0 - lam_init)
    branch_a = o.reshape(b, t, ATT_WIDTH).astype(x.dtype) @ p['w_branch_a']
    y_ssm, new_conv, new_ssm = ssm_branch(z, xbc, dt_raw, conv_state, ssm_state, p)
    branch_b = y_ssm @ p['w_branch_b']
    mix = jax.nn.sigmoid(g_a) * branch_a + jax.nn.sigmoid(g_b) * branch_b
    x = x + mix @ p['w_out']
    hm = rmsnorm(x, p['norm_mlp_g'])
    x = x + jnp.square(jax.nn.relu(hm @ p['w_up'])) @ p['w_down']
    return x, k, v, new_conv, new_ssm


def setup_inputs(seed: int = 0) -> dict:
    key = jax.random.key(seed)
    ks = jax.random.split(key, 32)
    n_pages = PAST_LEN // PAGE_SIZE
    n_used = DEC_BATCH * n_pages
    n_phys = (5 * n_used + 3) // 4

    def nrm(k, shape, scale):
        return jax.random.normal(k, shape, F32) * scale

    def gain(k, n):
        return 1.0 + nrm(k, (DEPTH, n), 0.02)

    dt0 = jnp.exp(jax.random.uniform(ks[20], (DEPTH, N_SSM_HEADS), F32, math.log(1e-3), math.log(1e-1)))
    return {
        'x_prompt': nrm(ks[0], (BATCH, SEQ, D_MODEL), 1.0),
        'x_sample': nrm(ks[1], (DEC_BATCH, DEC_SEQ, D_MODEL), 1.0),
        'cache_k': nrm(ks[2], (DEPTH, n_phys, PAGE_SIZE, N_HEADS, 2, HEAD_DIM), 1.0),
        'cache_v': nrm(ks[3], (DEPTH, n_phys, PAGE_SIZE, N_HEADS, V_DIM), 1.0),
        'state_conv': nrm(ks[4], (DEPTH, DEC_BATCH, CONV_W - 1, CONV_DIM), 1.0),
        'state_ssm': nrm(ks[5], (DEPTH, DEC_BATCH, N_SSM_HEADS, SSM_HEAD_DIM, D_STATE), 0.1),
        'page_table': jax.random.permutation(ks[6], n_phys)[:n_used].reshape(DEC_BATCH, n_pages).astype(jnp.int32),
        'norm_mix_g': gain(ks[7], D_MODEL),
        'w_in': nrm(ks[8], (DEPTH, D_MODEL, D_IN_PROJ), D_MODEL ** -0.5),
        'q_norm_g': gain(ks[9], HEAD_DIM),
        'k_norm_g': gain(ks[10], HEAD_DIM),
        'lambda_q1': nrm(ks[11], (DEPTH, HEAD_DIM), 0.1),
        'lambda_k1': nrm(ks[12], (DEPTH, HEAD_DIM), 0.1),
        'lambda_q2': nrm(ks[13], (DEPTH, HEAD_DIM), 0.1),
        'lambda_k2': nrm(ks[14], (DEPTH, HEAD_DIM), 0.1),
        'subln_g': gain(ks[15], V_DIM),
        'conv_w': nrm(ks[16], (DEPTH, CONV_W, CONV_DIM), CONV_W ** -0.5),
        'conv_b': nrm(ks[17], (DEPTH, CONV_DIM), 0.02),
        'dt_bias': dt0 + jnp.log(-jnp.expm1(-dt0)),
        'a_log': jnp.log(jax.random.uniform(ks[18], (DEPTH, N_SSM_HEADS), F32, 1.0, 16.0)),
        'd_skip': 1.0 + nrm(ks[19], (DEPTH, N_SSM_HEADS), 0.02),
        'ssm_norm_g': gain(ks[21], D_INNER),
        'w_branch_a': nrm(ks[22], (DEPTH, ATT_WIDTH, D_MODEL), ATT_WIDTH ** -0.5),
        'w_branch_b': nrm(ks[23], (DEPTH, D_INNER, D_MODEL), D_INNER ** -0.5),
        'w_out': nrm(ks[24], (DEPTH, D_MODEL, D_MODEL), D_MODEL ** -0.5),
        'norm_mlp_g': gain(ks[25], D_MODEL),
        'w_up': nrm(ks[26], (DEPTH, D_MODEL, D_FF), D_MODEL ** -0.5),
        'w_down': nrm(ks[27], (DEPTH, D_FF, D_MODEL), D_FF ** -0.5),
    }


def reference(x_prompt, x_sample, cache_k, cache_v, state_conv, state_ssm, page_table,
              norm_mix_g, w_in, q_norm_g, k_norm_g, lambda_q1, lambda_k1, lambda_q2, lambda_k2,
              subln_g, conv_w, conv_b, dt_bias, a_log, d_skip, ssm_norm_g,
              w_branch_a, w_branch_b, w_out, norm_mlp_g, w_up, w_down):
    bp, tp, _ = x_prompt.shape
    past = page_table.shape[1] * PAGE_SIZE
    pos_p = jnp.arange(tp)
    pos_s = past + jnp.arange(x_sample.shape[1])
    yp, ys = x_prompt, x_sample
    kp_l, vp_l, cp_l, sp_l, ks_l, vs_l, cs_l, ss_l = [], [], [], [], [], [], [], []
    for l in range(DEPTH):
        p = {
            'norm_mix_g': norm_mix_g[l], 'w_in': w_in[l], 'q_norm_g': q_norm_g[l], 'k_norm_g': k_norm_g[l],
            'lambda_q1': lambda_q1[l], 'lambda_k1': lambda_k1[l], 'lambda_q2': lambda_q2[l], 'lambda_k2': lambda_k2[l],
            'subln_g': subln_g[l], 'conv_w': conv_w[l], 'conv_b': conv_b[l], 'dt_bias': dt_bias[l],
            'a_log': a_log[l], 'd_skip': d_skip[l], 'ssm_norm_g': ssm_norm_g[l],
            'w_branch_a': w_branch_a[l], 'w_branch_b': w_branch_b[l], 'w_out': w_out[l],
            'norm_mlp_g': norm_mlp_g[l], 'w_up': w_up[l], 'w_down': w_down[l],
        }
        lam_init = 0.8 - 0.6 * math.exp(-0.3 * l)
        conv0 = jnp.zeros((bp, CONV_W - 1, CONV_DIM), x_prompt.dtype)
        ssm0 = jnp.zeros((bp, N_SSM_HEADS, SSM_HEAD_DIM, D_STATE), state_ssm.dtype)
        yp, kp, vp, cp, sp = mixer_layer(yp, pos_p, prompt_attention, conv0, ssm0, p, lam_init)
        attend_s = functools.partial(sample_attention, cache_k=cache_k, cache_v=cache_v,
                                     page_table=page_table, layer=l)
        ys, ks, vs, cs, ss = mixer_layer(ys, pos_s, attend_s, state_conv[l], state_ssm[l], p, lam_init)
        kp_l.append(kp); vp_l.append(vp); cp_l.append(cp); sp_l.append(sp)
        ks_l.append(ks); vs_l.append(vs); cs_l.append(cs); ss_l.append(ss)
    k_prompt = jnp.stack(kp_l)
    v_prompt = jnp.stack(vp_l)
    conv_prompt = jnp.stack(cp_l)
    ssm_prompt = jnp.stack(sp_l)
    k_sample = jnp.stack(ks_l)
    v_sample = jnp.stack(vs_l)
    conv_sample = jnp.stack(cs_l)
    ssm_sample = jnp.stack(ss_l)
    return (yp, ys, k_prompt, v_prompt, conv_prompt, ssm_prompt, k_sample, v_sample, conv_sample, ssm_sample)
```

```python
import functools
import math

import numpy as np
import jax
import jax.numpy as jnp
from jax import lax
from jax.experimental import pallas as pl
from jax.experimental.pallas import tpu as pltpu

F32 = jnp.float32
BF16 = jnp.bfloat16

D_MODEL = 1024
N_HEADS = 8
HEAD_DIM = 64
V_DIM = 2 * HEAD_DIM
ATT_WIDTH = N_HEADS * V_DIM
ROPE_THETA = 10000.0
D_INNER = 2 * D_MODEL
SSM_HEAD_DIM = 64
N_SSM_HEADS = D_INNER // SSM_HEAD_DIM
N_GROUPS = 4
HEADS_PER_GROUP = N_SSM_HEADS // N_GROUPS
GROUP_WIDTH = D_INNER // N_GROUPS
D_STATE = 128
CONV_W = 4
CONV_DIM = D_INNER + 2 * N_GROUPS * D_STATE
SSD_CHUNK = 128
D_FF = 4 * D_MODEL
PAGE_SIZE = 128
EPS = 1e-6
LOG2E = 1.4426950408889634

LANES = 128
SUBLANES = 8
VMEM_LIMIT_BYTES = 56 * 1024 * 1024

NEG_BIG = -1e30

PROJ_TILE = 1024
J_Q, J_K, J_V, J_Z0, J_X0, J_GA, J_GB = 0, 1, 2, 3, 5, 8, 9
N_PROJ_TILES = 10

PAGES_PER_STEP = 8
DEC_ROWS = 8


def _cparams(sem):
    return pltpu.CompilerParams(dimension_semantics=sem, vmem_limit_bytes=VMEM_LIMIT_BYTES)


def _dot(a, b):
    return jnp.dot(a, b, preferred_element_type=F32)


def _dot_nt(a, b):
    return lax.dot_general(a, b, (((1,), (1,)), ((), ())), preferred_element_type=F32)


def _exp2(x):
    return jnp.exp2(x)


def _rope_norm(acc, gain, cos, sin, bsel):
    lane = lax.broadcasted_iota(jnp.int32, (1, LANES), 1)
    first_half = (lane % HEAD_DIM) < (HEAD_DIM // 2)
    outs = []
    for hb in range(acc.shape[1] // LANES):
        xh = acc[:, hb * LANES:(hb + 1) * LANES]
        ss = _dot((xh * xh).astype(BF16), bsel)
        y = xh * lax.rsqrt(ss * (1.0 / HEAD_DIM) + EPS) * gain
        partner = jnp.where(first_half,
                            pltpu.roll(y, LANES - HEAD_DIM // 2, 1),
                            pltpu.roll(y, HEAD_DIM // 2, 1))
        outs.append(y * cos + partner * sin)
    return jnp.concatenate(outs, axis=-1)


def _in_proj_kernel(x_ref, g_ref, w_ref, wdt_ref, qg_ref, kg_ref, cos_ref, sin_ref, bsel_ref,
                    q_ref, kf_ref, kb_ref, vf_ref, vb_ref, z_ref, xbc_ref, dt_ref, ga_ref, gb_ref,
                    h_ref):
    j = pl.program_id(1)

    @pl.when(j == 0)
    def _():
        x = x_ref[...]
        ms = jnp.mean(x * x, axis=-1, keepdims=True)
        h = x * lax.rsqrt(ms + EPS) * g_ref[...]
        h_ref[...] = h.astype(BF16)
        dt_ref[...] = _dot(h_ref[...], wdt_ref[...])

    acc = _dot(h_ref[...], w_ref[...])

    @pl.when(j == J_Q)
    def _():
        r = _rope_norm(acc, qg_ref[...], cos_ref[...], sin_ref[...], bsel_ref[...])
        q_ref[...] = r.astype(BF16)

    @pl.when(j == J_K)
    def _():
        r = _rope_norm(acc, kg_ref[...], cos_ref[...], sin_ref[...], bsel_ref[...])
        kf_ref[...] = r
        kb_ref[...] = r.astype(BF16)

    @pl.when(j == J_V)
    def _():
        vf_ref[...] = acc
        vb_ref[...] = acc.astype(BF16)

    @pl.when((j >= J_Z0) & (j < J_X0))
    def _():
        z_ref[...] = acc

    @pl.when((j >= J_X0) & (j < J_GA))
    def _():
        xbc_ref[...] = acc

    @pl.when(j == J_GA)
    def _():
        ga_ref[...] = acc

    @pl.when(j == J_GB)
    def _():
        gb_ref[...] = acc


def _in_proj(x, g, w_main, w_dt, qg, kg, cos_t, sin_t, bsel, tm):
    n = x.shape[0]
    rope_blocks = cos_t.shape[0] // tm
    grid = (n // tm, N_PROJ_TILES)
    row = lambda i, j: (i, 0)
    const = lambda i, j: (0, 0)
    rope_map = lambda i, j: (i % rope_blocks, 0)
    in_specs = [
        pl.BlockSpec((tm, D_MODEL), row),
        pl.BlockSpec((1, D_MODEL), const),
        pl.BlockSpec((D_MODEL, PROJ_TILE), lambda i, j: (0, j)),
        pl.BlockSpec((D_MODEL, LANES), const),
        pl.BlockSpec((1, LANES), const),
        pl.BlockSpec((1, LANES), const),
        pl.BlockSpec((tm, LANES), rope_map),
        pl.BlockSpec((tm, LANES), rope_map),
        pl.BlockSpec((LANES, LANES), const),
    ]
    out_shape = [
        jax.ShapeDtypeStruct((n, ATT_WIDTH), BF16),
        jax.ShapeDtypeStruct((n, ATT_WIDTH), F32),
        jax.ShapeDtypeStruct((n, ATT_WIDTH), BF16),
        jax.ShapeDtypeStruct((n, ATT_WIDTH), F32),
        jax.ShapeDtypeStruct((n, ATT_WIDTH), BF16),
        jax.ShapeDtypeStruct((n, D_INNER), F32),
        jax.ShapeDtypeStruct((n, CONV_DIM), F32),
        jax.ShapeDtypeStruct((n, LANES), F32),
        jax.ShapeDtypeStruct((n, D_MODEL), F32),
        jax.ShapeDtypeStruct((n, D_MODEL), F32),
    ]
    out_specs = [
        pl.BlockSpec((tm, PROJ_TILE), row),
        pl.BlockSpec((tm, PROJ_TILE), row),
        pl.BlockSpec((tm, PROJ_TILE), row),
        pl.BlockSpec((tm, PROJ_TILE), row),
        pl.BlockSpec((tm, PROJ_TILE), row),
        pl.BlockSpec((tm, PROJ_TILE), lambda i, j: (i, jnp.clip(j - J_Z0, 0, J_X0 - J_Z0 - 1))),
        pl.BlockSpec((tm, PROJ_TILE), lambda i, j: (i, jnp.clip(j - J_X0, 0, J_GA - J_X0 - 1))),
        pl.BlockSpec((tm, LANES), row),
        pl.BlockSpec((tm, PROJ_TILE), row),
        pl.BlockSpec((tm, PROJ_TILE), row),
    ]
    return pl.pallas_call(
        _in_proj_kernel,
        out_shape=out_shape,
        grid=grid,
        in_specs=in_specs,
        out_specs=out_specs,
        scratch_shapes=[pltpu.VMEM((tm, D_MODEL), BF16)],
        compiler_params=_cparams(("parallel", "arbitrary")),
        name="in_proj",
    )(x, g, w_main, w_dt, qg, kg, cos_t, sin_t, bsel)


def _flash_update(s, v, m, l, acc):
    m_new = jnp.maximum(m, jnp.max(s, axis=-1, keepdims=True))
    alpha = _exp2(m - m_new)
    p = _exp2(s - m_new)
    l = alpha * l + jnp.sum(p, axis=-1, keepdims=True)
    acc = alpha * acc + _dot(p.astype(BF16), v)
    return m_new, l, acc


def _subln(o, g, out_scale):
    ms = jnp.mean(o * o, axis=-1, keepdims=True)
    return o * lax.rsqrt(ms + EPS) * g * out_scale


def _prompt_attn_kernel(lam_ref, q_ref, k_ref, v_ref, g_ref, o_ref, *, blk, out_scale):
    qi = pl.program_id(2)
    q = q_ref[0]
    qs = (q[:, :HEAD_DIM], q[:, HEAD_DIM:])

    def step(kb, carry, masked):
        start = pl.multiple_of(kb * blk, blk)
        kblk = k_ref[0, pl.ds(start, blk), :]
        vblk = v_ref[0, pl.ds(start, blk), :]
        out = []
        for c in range(2):
            m, l, acc = carry[c]
            s = _dot_nt(qs[c], kblk[:, c * HEAD_DIM:(c + 1) * HEAD_DIM])
            if masked:
                row = lax.broadcasted_iota(jnp.int32, s.shape, 0)
                col = lax.broadcasted_iota(jnp.int32, s.shape, 1)
                s = jnp.where(col <= row, s, NEG_BIG)
            out.append(_flash_update(s, vblk, m, l, acc))
        return tuple(out)

    init = tuple((jnp.full((blk, 1), NEG_BIG, F32), jnp.zeros((blk, 1), F32),
                  jnp.zeros((blk, V_DIM), F32)) for _ in range(2))
    carry = lax.fori_loop(0, qi, lambda kb, c: step(kb, c, False), init)
    (m0, l0, a0), (m1, l1, a1) = step(qi, carry, True)
    lam = lam_ref[0, 0]
    o = a0 / l0 - lam * (a1 / l1)
    o_ref[0] = _subln(o, g_ref[...], out_scale).astype(o_ref.dtype)


def _prompt_attention(q, k, v, lam, subln_g, out_scale, blk):
    b, t, _ = q.shape
    grid = (b, N_HEADS, t // blk)
    kern = functools.partial(_prompt_attn_kernel, blk=blk, out_scale=out_scale)
    return pl.pallas_call(
        kern,
        out_shape=jax.ShapeDtypeStruct((b, t, ATT_WIDTH), BF16),
        grid=grid,
        in_specs=[
            pl.BlockSpec(memory_space=pltpu.SMEM),
            pl.BlockSpec((1, blk, V_DIM), lambda bi, h, qi: (bi, qi, h)),
            pl.BlockSpec((1, t, V_DIM), lambda bi, h, qi: (bi, 0, h)),
            pl.BlockSpec((1, t, V_DIM), lambda bi, h, qi: (bi, 0, h)),
            pl.BlockSpec((1, V_DIM), lambda bi, h, qi: (0, 0)),
        ],
        out_specs=pl.BlockSpec((1, blk, V_DIM), lambda bi, h, qi: (bi, qi, h)),
        compiler_params=_cparams(("parallel", "parallel", "arbitrary")),
        name="prompt_attention",
    )(lam, q, k, v, subln_g)


def _sample_attn_kernel(pt_ref, lam_ref, wq_ref, knew_ref, vnew_ref, g_ref, hmask_ref, *rest,
                        n_valid, out_scale):
    kp_refs = rest[:PAGES_PER_STEP]
    vp_refs = rest[PAGES_PER_STEP:2 * PAGES_PER_STEP]
    o_ref = rest[2 * PAGES_PER_STEP]
    m_ref, l_ref, acc_ref = rest[2 * PAGES_PER_STEP + 1:]
    j = pl.program_id(1)
    wq = wq_ref[0]

    @pl.when(j == 0)
    def _():
        m_ref[...] = jnp.full(m_ref.shape, NEG_BIG, F32)
        l_ref[...] = jnp.zeros(l_ref.shape, F32)
        acc_ref[...] = jnp.zeros(acc_ref.shape, F32)

    s = jnp.concatenate([_dot_nt(wq, kp_refs[p][0, 0]) for p in range(PAGES_PER_STEP)], axis=-1)
    m = m_ref[...]
    m_new = jnp.maximum(m, jnp.max(s, axis=-1, keepdims=True))
    alpha = _exp2(m - m_new)
    pr = _exp2(s - m_new)
    l_ref[...] = alpha * l_ref[...] + jnp.sum(pr, axis=-1, keepdims=True)
    pv = _dot(pr[:, :PAGE_SIZE], vp_refs[0][0, 0])
    for p in range(1, PAGES_PER_STEP):
        pv = pv + _dot(pr[:, p * PAGE_SIZE:(p + 1) * PAGE_SIZE], vp_refs[p][0, 0])
    acc_ref[...] = alpha * acc_ref[...] + pv
    m_ref[...] = m_new

    @pl.when(j == pl.num_programs(1) - 1)
    def _():
        sn = _dot_nt(wq, knew_ref[0])
        row_t = lax.broadcasted_iota(jnp.int32, sn.shape, 0) % DEC_ROWS
        col_t = lax.broadcasted_iota(jnp.int32, sn.shape, 1)
        sn = jnp.where((col_t <= row_t) & (col_t < n_valid), sn, NEG_BIG)
        mm = m_ref[...]
        mf = jnp.maximum(mm, jnp.max(sn, axis=-1, keepdims=True))
        al = _exp2(mm - mf)
        pn = _exp2(sn - mf)
        lf = al * l_ref[...] + jnp.sum(pn, axis=-1, keepdims=True)
        af = al * acc_ref[...] + _dot(pn, vnew_ref[0])
        of = af / lf
        half = N_HEADS * DEC_ROWS
        lam = lam_ref[0, 0]
        od = of[:half] - lam * of[half:]
        od = od.reshape(N_HEADS, DEC_ROWS, ATT_WIDTH) * hmask_ref[...]
        o = jnp.sum(od, axis=0)
        outs = []
        for h in range(N_HEADS):
            outs.append(_subln(o[:, h * V_DIM:(h + 1) * V_DIM], g_ref[...], out_scale))
        o_ref[0] = jnp.concatenate(outs, axis=-1)


def _sample_attention(page_table, lam, wq, k_new, v_new, subln_g, cache_k, cache_v, layer,
                      n_valid, out_scale):
    db, n_pages = page_table.shape
    n_steps = n_pages // PAGES_PER_STEP
    rows = wq.shape[1]
    hmask = (np.arange(ATT_WIDTH)[None, :] // V_DIM == np.arange(N_HEADS)[:, None])
    hmask = jnp.asarray(hmask.astype(np.float32).reshape(N_HEADS, 1, ATT_WIDTH))

    def page_spec(p):
        return pl.BlockSpec(
            (1, 1, PAGE_SIZE, ATT_WIDTH),
            lambda b, j, pt: (layer, pt[b * n_pages + j * PAGES_PER_STEP + p], 0, 0))

    kern = functools.partial(_sample_attn_kernel, n_valid=n_valid, out_scale=out_scale)
    grid_spec = pltpu.PrefetchScalarGridSpec(
        num_scalar_prefetch=1,
        grid=(db, n_steps),
        in_specs=[
            pl.BlockSpec(memory_space=pltpu.SMEM),
            pl.BlockSpec((1, rows, ATT_WIDTH), lambda b, j, pt: (b, 0, 0)),
            pl.BlockSpec((1, DEC_ROWS, ATT_WIDTH), lambda b, j, pt: (b, 0, 0)),
            pl.BlockSpec((1, DEC_ROWS, ATT_WIDTH), lambda b, j, pt: (b, 0, 0)),
            pl.BlockSpec((1, V_DIM), lambda b, j, pt: (0, 0)),
            pl.BlockSpec((N_HEADS, 1, ATT_WIDTH), lambda b, j, pt: (0, 0, 0)),
        ] + [page_spec(p) for p in range(PAGES_PER_STEP)]
          + [page_spec(p) for p in range(PAGES_PER_STEP)],
        out_specs=pl.BlockSpec((1, DEC_ROWS, ATT_WIDTH), lambda b, j, pt: (b, 0, 0)),
        scratch_shapes=[
            pltpu.VMEM((rows, 1), F32),
            pltpu.VMEM((rows, 1), F32),
            pltpu.VMEM((rows, ATT_WIDTH), F32),
        ],
    )
    return pl.pallas_call(
        kern,
        out_shape=jax.ShapeDtypeStruct((db, DEC_ROWS, ATT_WIDTH), F32),
        grid_spec=grid_spec,
        compiler_params=_cparams(("parallel", "arbitrary")),
        name="sample_attention",
    )(page_table.reshape(-1), lam, wq, k_new, v_new, subln_g, hmask,
      *([cache_k] * PAGES_PER_STEP), *([cache_v] * PAGES_PER_STEP))


def _split3(v):
    hi = v.astype(BF16)
    r1 = v - hi.astype(F32)
    mid = r1.astype(BF16)
    lo = (r1 - mid.astype(F32)).astype(BF16)
    return hi, mid, lo


def _softplus(x):
    return jnp.maximum(x, 0.0) + jnp.log1p(jnp.exp(-jnp.abs(x)))


def _ssd_kernel(*refs, chunk, n_valid, has_init):
    (xbc_ref, z_ref, dt_ref, dtt_ref, cinit_ref) = refs[:5]
    k = 5
    if has_init:
        sinit_ref = refs[k]
        k += 1
    (cw_ref, cb_ref, bias_ref, biast_ref, alog_ref, alogt_ref, dskip_ref, ng_ref,
     ltri_ref, utri_ref, e64_ref, e128_ref) = refs[k:k + 12]
    y_ref, sout_ref = refs[k + 12:k + 14]
    xp_ref, st_ref = refs[k + 14:]
    c = pl.program_id(1)
    L = chunk

    @pl.when(c == 0)
    def _():
        xp_ref[0:SUBLANES, :] = cinit_ref[0]
        if has_init:
            st_ref[...] = sinit_ref[0].T
        else:
            st_ref[...] = jnp.zeros(st_ref.shape, F32)

    xraw = xbc_ref[0]
    xp_ref[SUBLANES:, :] = xraw
    conv = cb_ref[...]
    for kk in range(CONV_W - 1):
        off = SUBLANES - (CONV_W - 1) + kk
        conv = conv + xp_ref[off:off + L, :] * cw_ref[kk:kk + 1, :]
    conv = conv + xraw * cw_ref[CONV_W - 1:CONV_W, :]
    xp_ref[0:SUBLANES, :] = xraw[L - SUBLANES:, :]
    xact = conv * jax.nn.sigmoid(conv)
    xs = xact[:, :D_INNER]

    nh = N_SSM_HEADS
    dt = _softplus(dt_ref[0][:, :nh] + bias_ref[...])
    dtt = _softplus(dtt_ref[0] + biast_ref[...])
    if n_valid is not None:
        tok = c * L + lax.broadcasted_iota(jnp.int32, (L, 1), 0)
        dt = jnp.where(tok < n_valid, dt, 0.0)
        tokt = c * L + lax.broadcasted_iota(jnp.int32, (1, L), 1)
        dtt = jnp.where(tokt < n_valid, dtt, 0.0)
    da = dt * (-jnp.exp(alog_ref[...]))
    dat = dtt * (-jnp.exp(alogt_ref[...]))
    ltri = ltri_ref[...]
    utri = utri_ref[...]
    a = sum(_dot(ltri, part) for part in _split3(da))
    at = sum(_dot(part, utri) for part in _split3(dat))
    a_last = a[L - 1:L, :]
    w = jnp.exp(a_last - a) * dt

    def expand(v, e_ref):
        return _dot(jnp.concatenate(_split3(v), axis=-1), e_ref[...])

    a128 = expand(a, e128_ref)
    a64 = expand(a, e64_ref)
    w64 = expand(w, e64_ref)
    ea64 = jnp.exp(a64)
    xw = (xs * w64).astype(BF16)

    row = lax.broadcasted_iota(jnp.int32, (L, L), 0)
    col = lax.broadcasted_iota(jnp.int32, (L, L), 1)
    causal = col <= row
    lane = lax.broadcasted_iota(jnp.int32, (1, LANES), 1)
    low = lane < SSM_HEAD_DIM

    ys = []
    for g in range(N_GROUPS):
        bg = xact[:, D_INNER + g * D_STATE:D_INNER + (g + 1) * D_STATE]
        cg = xact[:, D_INNER + (N_GROUPS + g) * D_STATE:D_INNER + (N_GROUPS + g + 1) * D_STATE]
        bgb = bg.astype(BF16)
        cgb = cg.astype(BF16)
        gmat = _dot_nt(cgb, bgb)
        gs = slice(g * GROUP_WIDTH, (g + 1) * GROUP_WIDTH)
        st_g = st_ref[:, gs]
        y_off = _dot(cgb, st_g.astype(BF16)) * ea64[:, gs]
        y_diag = []
        for pr in range(HEADS_PER_GROUP // 2):
            ms = []
            for hh in range(2):
                h = g * HEADS_PER_GROUP + 2 * pr + hh
                seg = a128[:, h * LANES:(h + 1) * LANES] - at[h:h + 1, :]
                lm = jnp.where(causal, jnp.exp(seg), 0.0)
                ms.append((gmat * lm * dtt[h:h + 1, :]).astype(BF16))
            h0 = g * HEADS_PER_GROUP + 2 * pr
            xpair = xs[:, h0 * SSM_HEAD_DIM:(h0 + 2) * SSM_HEAD_DIM]
            xstack = jnp.concatenate([jnp.where(low, xpair, 0.0), jnp.where(low, 0.0, xpair)],
                                     axis=0).astype(BF16)
            y_diag.append(_dot(jnp.concatenate(ms, axis=-1), xstack))
        ys.append(jnp.concatenate(y_diag, axis=-1) + y_off)
        cd = ea64[L - 1:L, gs]
        st_ref[:, gs] = cd * st_g + _dot(bg.T.astype(BF16), xw[:, gs])
    y = jnp.concatenate(ys, axis=-1) + dskip_ref[...] * xs
    zz = z_ref[0]
    y = y * (zz * jax.nn.sigmoid(zz))
    outs = []
    for g in range(N_GROUPS):
        yg = y[:, g * GROUP_WIDTH:(g + 1) * GROUP_WIDTH]
        ms = jnp.mean(yg * yg, axis=-1, keepdims=True)
        outs.append(yg * lax.rsqrt(ms + EPS) * ng_ref[:, g * GROUP_WIDTH:(g + 1) * GROUP_WIDTH])
    y_ref[0] = jnp.concatenate(outs, axis=-1).astype(y_ref.dtype)

    @pl.when(c == pl.num_programs(1) - 1)
    def _():
        sout_ref[0] = st_ref[...].T


def _ssd_constants(chunk):
    idx = np.arange(chunk)
    ltri = (idx[None, :] <= idx[:, None]).astype(np.float32)
    utri = ltri.T
    k3 = np.arange(3 * N_SSM_HEADS) % N_SSM_HEADS
    e64 = (k3[:, None] == (np.arange(D_INNER)[None, :] // SSM_HEAD_DIM)).astype(np.float32)
    e128 = (k3[:, None] == (np.arange(N_SSM_HEADS * LANES)[None, :] // LANES)).astype(np.float32)
    return tuple(jnp.asarray(m, dtype=BF16) for m in (ltri, utri, e64, e128))


def _ssd(xbc, z, dt_raw, conv_init, ssm_init, p, n_valid):
    b, t, _ = xbc.shape
    chunk = SSD_CHUNK
    nc = t // chunk
    has_init = ssm_init is not None
    dtt = jnp.swapaxes(dt_raw[:, :, :N_SSM_HEADS], 1, 2)
    cinit = jnp.zeros((b, SUBLANES, CONV_DIM), F32)
    cinit = cinit.at[:, SUBLANES - (CONV_W - 1):, :].set(conv_init)
    consts = _ssd_constants(chunk)
    bmap = lambda bi, c: (bi, c, 0)
    const2 = lambda bi, c: (0, 0)
    args = [xbc, z, dt_raw, dtt, cinit]
    in_specs = [
        pl.BlockSpec((1, chunk, CONV_DIM), bmap),
        pl.BlockSpec((1, chunk, D_INNER), bmap),
        pl.BlockSpec((1, chunk, LANES), bmap),
        pl.BlockSpec((1, N_SSM_HEADS, chunk), lambda bi, c: (bi, 0, c)),
        pl.BlockSpec((1, SUBLANES, CONV_DIM), lambda bi, c: (bi, 0, 0)),
    ]
    if has_init:
        args.append(ssm_init.reshape(b, D_INNER, D_STATE))
        in_specs.append(pl.BlockSpec((1, D_INNER, D_STATE), lambda bi, c: (bi, 0, 0)))
    params = [p['conv_w'], p['conv_b'][None, :], p['dt_bias'][None, :], p['dt_bias'][:, None],
              p['a_log'][None, :], p['a_log'][:, None],
              jnp.repeat(p['d_skip'], SSM_HEAD_DIM)[None, :], p['ssm_norm_g'][None, :]]
    args += params + list(consts)
    in_specs += [pl.BlockSpec(a.shape, const2) for a in params + list(consts)]
    kern = functools.partial(_ssd_kernel, chunk=chunk, n_valid=n_valid, has_init=has_init)
    y, s_out = pl.pallas_call(
        kern,
        out_shape=[jax.ShapeDtypeStruct((b, t, D_INNER), BF16),
                   jax.ShapeDtypeStruct((b, D_INNER, D_STATE), F32)],
        grid=(b, nc),
        in_specs=in_specs,
        out_specs=[pl.BlockSpec((1, chunk, D_INNER), bmap),
                   pl.BlockSpec((1, D_INNER, D_STATE), lambda bi, c: (bi, 0, 0))],
        scratch_shapes=[pltpu.VMEM((chunk + SUBLANES, CONV_DIM), F32),
                        pltpu.VMEM((D_STATE, D_INNER), F32)],
        compiler_params=_cparams(("parallel", "arbitrary")),
        name="ssd",
    )(*args)
    return y, s_out.reshape(b, N_SSM_HEADS, SSM_HEAD_DIM, D_STATE)


def _merge_kernel(o_ref, y_ref, ga_ref, gb_ref, x_ref, wa_ref, wb_ref, wo_ref, out_ref):
    ba = _dot(o_ref[...], wa_ref[...])
    bb = _dot(y_ref[...], wb_ref[...])
    mix = jax.nn.sigmoid(ga_ref[...]) * ba + jax.nn.sigmoid(gb_ref[...]) * bb
    out_ref[...] = x_ref[...] + _dot(mix.astype(BF16), wo_ref[...])


def _merge(o, y, ga, gb, x, wa, wb, wo, tm):
    n = x.shape[0]
    row = lambda i: (i, 0)
    const = lambda i: (0, 0)
    return pl.pallas_call(
        _merge_kernel,
        out_shape=jax.ShapeDtypeStruct((n, D_MODEL), F32),
        grid=(n // tm,),
        in_specs=[
            pl.BlockSpec((tm, ATT_WIDTH), row),
            pl.BlockSpec((tm, D_INNER), row),
            pl.BlockSpec((tm, D_MODEL), row),
            pl.BlockSpec((tm, D_MODEL), row),
            pl.BlockSpec((tm, D_MODEL), row),
            pl.BlockSpec((ATT_WIDTH, D_MODEL), const),
            pl.BlockSpec((D_INNER, D_MODEL), const),
            pl.BlockSpec((D_MODEL, D_MODEL), const),
        ],
        out_specs=pl.BlockSpec((tm, D_MODEL), row),
        compiler_params=_cparams(("parallel",)),
        name="merge",
    )(o, y, ga, gb, x, wa, wb, wo)


def _mlp_kernel(x_ref, g_ref, wu_ref, wd_ref, out_ref, h_ref, acc_ref):
    k = pl.program_id(1)

    @pl.when(k == 0)
    def _():
        x = x_ref[...]
        ms = jnp.mean(x * x, axis=-1, keepdims=True)
        h_ref[...] = (x * lax.rsqrt(ms + EPS) * g_ref[...]).astype(BF16)
        acc_ref[...] = x

    u = _dot(h_ref[...], wu_ref[...])
    r = jnp.square(jnp.maximum(u, 0.0))
    acc_ref[...] += _dot(r.astype(BF16), wd_ref[...])

    @pl.when(k == pl.num_programs(1) - 1)
    def _():
        out_ref[...] = acc_ref[...]


def _mlp(x, g, wu, wd, tm, tk):
    n = x.shape[0]
    return pl.pallas_call(
        _mlp_kernel,
        out_shape=jax.ShapeDtypeStruct((n, D_MODEL), F32),
        grid=(n // tm, D_FF // tk),
        in_specs=[
            pl.BlockSpec((tm, D_MODEL), lambda i, k: (i, 0)),
            pl.BlockSpec((1, D_MODEL), lambda i, k: (0, 0)),
            pl.BlockSpec((D_MODEL, tk), lambda i, k: (0, k)),
            pl.BlockSpec((tk, D_MODEL), lambda i, k: (k, 0)),
        ],
        out_specs=pl.BlockSpec((tm, D_MODEL), lambda i, k: (i, 0)),
        scratch_shapes=[pltpu.VMEM((tm, D_MODEL), BF16), pltpu.VMEM((tm, D_MODEL), F32)],
        compiler_params=_cparams(("parallel", "arbitrary")),
        name="mlp",
    )(x, g, wu, wd)


def _rope_tables(pos):
    half = HEAD_DIM // 2
    inv = ROPE_THETA ** (-jnp.arange(half, dtype=F32) / half)
    ang = pos.astype(F32)[:, None] * inv[None, :]
    cos, sin = jnp.cos(ang), jnp.sin(ang)
    cos128 = jnp.tile(cos, (1, LANES // half))
    sin128 = jnp.tile(jnp.concatenate([-sin, sin], axis=-1), (1, LANES // HEAD_DIM))
    return cos128, sin128


def _pack_weights(p):
    sizes = (ATT_WIDTH, ATT_WIDTH, ATT_WIDTH, D_INNER, CONV_DIM, N_SSM_HEADS, D_MODEL, D_MODEL)
    offs = np.concatenate([[0], np.cumsum(sizes)])
    sec = [p['w_in'][:, offs[i]:offs[i + 1]] for i in range(len(sizes))]
    w_main = jnp.concatenate(sec[:5] + sec[6:], axis=1).astype(BF16)
    w_dt = jnp.pad(sec[5], ((0, 0), (0, LANES - N_SSM_HEADS))).astype(BF16)
    return dict(
        w_main=w_main, w_dt=w_dt,
        wa=p['w_branch_a'].astype(BF16), wb=p['w_branch_b'].astype(BF16),
        wo=p['w_out'].astype(BF16), wu=p['w_up'].astype(BF16), wd=p['w_down'].astype(BF16),
    )


def _mixer_common(x2d, pos_tables, p, w, tm):
    idx = np.arange(LANES)
    bsel = jnp.asarray((idx[:, None] // HEAD_DIM == idx[None, :] // HEAD_DIM).astype(np.float32),
                       dtype=BF16)
    q_scale = (HEAD_DIM ** -0.5) * LOG2E
    qg = (jnp.tile(p['q_norm_g'], LANES // HEAD_DIM) * q_scale)[None, :]
    kg = jnp.tile(p['k_norm_g'], LANES // HEAD_DIM)[None, :]
    cos_t, sin_t = pos_tables
    return _in_proj(x2d, p['norm_mix_g'][None, :], w['w_main'], w['w_dt'], qg, kg, cos_t, sin_t,
                    bsel, tm)


def _finish(o, y_ssm, ga, gb, x2d, p, w, tm):
    x1 = _merge(o, y_ssm, ga, gb, x2d, w['wa'], w['wb'], w['wo'], tm)
    return _mlp(x1, p['norm_mlp_g'][None, :], w['wu'], w['wd'], tm, 1024)


def _lam(p, lam_init):
    lam = (jnp.exp(jnp.sum(p['lambda_q1'].astype(F32) * p['lambda_k1'].astype(F32)))
           - jnp.exp(jnp.sum(p['lambda_q2'].astype(F32) * p['lambda_k2'].astype(F32))) + lam_init)
    return lam.reshape(1, 1).astype(F32)


def _prompt_layer(x, p, w, lam, lam_init):
    b, t, _ = x.shape
    n = b * t
    tm = min(512, t)
    x2d = x.reshape(n, D_MODEL)
    tables = _rope_tables(jnp.arange(t))
    q, kf, kb, vf, vb, z, xbc, dt, ga, gb = _mixer_common(x2d, tables, p, w, tm)
    blk = min(256, t)
    o = _prompt_attention(q.reshape(b, t, -1), kb.reshape(b, t, -1), vb.reshape(b, t, -1), lam,
                          p['subln_g'][None, :], 1.0 - lam_init, blk)
    xbc3 = xbc.reshape(b, t, CONV_DIM)
    conv0 = jnp.zeros((b, CONV_W - 1, CONV_DIM), F32)
    y_ssm, new_ssm = _ssd(xbc3, z.reshape(b, t, D_INNER), dt.reshape(b, t, LANES), conv0, None, p,
                          None)
    y = _finish(o.reshape(n, -1), y_ssm.reshape(n, -1), ga, gb, x2d, p, w, tm)
    new_conv = jnp.concatenate([conv0, xbc3], axis=1)[:, -(CONV_W - 1):]
    return (y.reshape(b, t, D_MODEL), kf.reshape(b, t, N_HEADS, 2, HEAD_DIM),
            vf.reshape(b, t, N_HEADS, V_DIM), new_conv, new_ssm)


def _sample_layer(x, p, w, lam, lam_init, cache_k, cache_v, page_table, layer, conv_state,
                  ssm_state):
    b, t, _ = x.shape
    assert t <= DEC_ROWS
    n = b * t
    tm = n
    past = page_table.shape[1] * PAGE_SIZE
    x2d = x.reshape(n, D_MODEL)
    cos_t, sin_t = _rope_tables(past + jnp.arange(t))
    tables = (jnp.tile(cos_t, (b, 1)), jnp.tile(sin_t, (b, 1)))
    q, kf, kb, vf, vb, z, xbc, dt, ga, gb = _mixer_common(x2d, tables, p, w, tm)

    q5 = q.reshape(b, t, N_HEADS, 2, HEAD_DIM)
    q5 = jnp.pad(q5, ((0, 0), (0, DEC_ROWS - t), (0, 0), (0, 0), (0, 0)))
    eye_h = jnp.eye(N_HEADS, dtype=BF16)
    eye_c = jnp.eye(2, dtype=BF16)
    wq = jnp.einsum('bthcd,hH,cC->bchtHCd', q5, eye_h, eye_c).reshape(
        b, 2 * N_HEADS * DEC_ROWS, ATT_WIDTH).astype(F32)
    pad_t = lambda a: jnp.pad(a.reshape(b, t, -1), ((0, 0), (0, DEC_ROWS - t), (0, 0)))
    ck = cache_k.reshape(cache_k.shape[0], cache_k.shape[1], PAGE_SIZE, ATT_WIDTH)
    cv = cache_v.reshape(cache_v.shape[0], cache_v.shape[1], PAGE_SIZE, ATT_WIDTH)
    o = _sample_attention(page_table, lam, wq, pad_t(kf), pad_t(vf), p['subln_g'][None, :], ck, cv,
                          layer, t, 1.0 - lam_init)
    o = o[:, :t].reshape(n, ATT_WIDTH).astype(BF16)

    tpad = SSD_CHUNK
    pad_c = lambda a: jnp.pad(a.reshape(b, t, -1), ((0, 0), (0, tpad - t), (0, 0)))
    xbc3 = xbc.reshape(b, t, CONV_DIM)
    y_ssm, new_ssm = _ssd(pad_c(xbc), pad_c(z), pad_c(dt), conv_state, ssm_state, p, t)
    y_ssm = y_ssm[:, :t].reshape(n, D_INNER)
    y = _finish(o, y_ssm, ga, gb, x2d, p, w, tm)
    new_conv = jnp.concatenate([conv_state, xbc3], axis=1)[:, -(CONV_W - 1):]
    return (y.reshape(b, t, D_MODEL), kf.reshape(b, t, N_HEADS, 2, HEAD_DIM),
            vf.reshape(b, t, N_HEADS, V_DIM), new_conv, new_ssm)


def kernel(x_prompt, x_sample, cache_k, cache_v, state_conv, state_ssm, page_table, norm_mix_g, w_in, q_norm_g, k_norm_g, lambda_q1, lambda_k1, lambda_q2, lambda_k2, subln_g, conv_w, conv_b, dt_bias, a_log, d_skip, ssm_norm_g, w_branch_a, w_branch_b, w_out, norm_mlp_g, w_up, w_down):
    depth = w_in.shape[0]
    yp, ys = x_prompt, x_sample
    outs = [[] for _ in range(8)]
    for l in range(depth):
        p = {
            'norm_mix_g': norm_mix_g[l], 'w_in': w_in[l], 'q_norm_g': q_norm_g[l], 'k_norm_g': k_norm_g[l],
            'lambda_q1': lambda_q1[l], 'lambda_k1': lambda_k1[l], 'lambda_q2': lambda_q2[l], 'lambda_k2': lambda_k2[l],
            'subln_g': subln_g[l], 'conv_w': conv_w[l], 'conv_b': conv_b[l], 'dt_bias': dt_bias[l],
            'a_log': a_log[l], 'd_skip': d_skip[l], 'ssm_norm_g': ssm_norm_g[l],
            'w_branch_a': w_branch_a[l], 'w_branch_b': w_branch_b[l], 'w_out': w_out[l],
            'norm_mlp_g': norm_mlp_g[l], 'w_up': w_up[l], 'w_down': w_down[l],
        }
        lam_init = 0.8 - 0.6 * math.exp(-0.3 * l)
        lam = _lam(p, lam_init)
        w = _pack_weights(p)
        yp, kp, vp, cp, sp = _prompt_layer(yp, p, w, lam, lam_init)
        ys, ks, vs, cs, ss = _sample_layer(ys, p, w, lam, lam_init, cache_k, cache_v, page_table, l,
                                           state_conv[l], state_ssm[l])
        for lst, val in zip(outs, (kp, vp, cp, sp, ks, vs, cs, ss)):
            lst.append(val)
    return (yp, ys) + tuple(jnp.stack(o) for o in outs)
```

```python
import functools
import math

import numpy as np
import jax
import jax.numpy as jnp
from jax import lax
from jax.experimental import pallas as pl
from jax.experimental.pallas import tpu as pltpu

F32 = jnp.float32
BF16 = jnp.bfloat16

D_MODEL = 1024
N_HEADS = 8
HEAD_DIM = 64
V_DIM = 2 * HEAD_DIM
ATT_WIDTH = N_HEADS * V_DIM
ROPE_THETA = 10000.0
D_INNER = 2 * D_MODEL
SSM_HEAD_DIM = 64
N_SSM_HEADS = D_INNER // SSM_HEAD_DIM
N_GROUPS = 4
HEADS_PER_GROUP = N_SSM_HEADS // N_GROUPS
GROUP_WIDTH = D_INNER // N_GROUPS
D_STATE = 128
CONV_W = 4
CONV_DIM = D_INNER + 2 * N_GROUPS * D_STATE
SSD_CHUNK = 128
D_FF = 4 * D_MODEL
PAGE_SIZE = 128
EPS = 1e-6
LOG2E = 1.4426950408889634

LANES = 128
SUBLANES = 8
VMEM_LIMIT_BYTES = 56 * 1024 * 1024

NEG_BIG = -1e30

PROJ_TILE = 512
_SECTION_WIDTHS = (ATT_WIDTH, ATT_WIDTH, ATT_WIDTH, D_INNER, CONV_DIM, D_MODEL, D_MODEL)
_SECTION_STARTS = tuple(int(s) // PROJ_TILE for s in np.cumsum((0,) + _SECTION_WIDTHS))
J_Q, J_K, J_V, J_Z, J_X, J_GA, J_GB, N_PROJ_TILES = _SECTION_STARTS

PAGES_PER_STEP = 8
DEC_ROWS = 8
QROWS = 2 * DEC_ROWS


def _cparams(sem):
    return pltpu.CompilerParams(dimension_semantics=sem, vmem_limit_bytes=VMEM_LIMIT_BYTES)


def _dot(a, b):
    return jnp.dot(a, b, preferred_element_type=F32)


def _dot_nt(a, b):
    return lax.dot_general(a, b, (((1,), (1,)), ((), ())), preferred_element_type=F32)


def _exp2(x):
    return jnp.exp2(x)


def _rope_norm(acc, gain, cos, sin, bsel):
    lane = lax.broadcasted_iota(jnp.int32, (1, LANES), 1)
    first_half = (lane % HEAD_DIM) < (HEAD_DIM // 2)
    outs = []
    for hb in range(acc.shape[1] // LANES):
        xh = acc[:, hb * LANES:(hb + 1) * LANES]
        ss = _dot((xh * xh).astype(BF16), bsel)
        y = xh * lax.rsqrt(ss * (1.0 / HEAD_DIM) + EPS) * gain
        partner = jnp.where(first_half,
                            pltpu.roll(y, LANES - HEAD_DIM // 2, 1),
                            pltpu.roll(y, HEAD_DIM // 2, 1))
        outs.append(y * cos + partner * sin)
    return jnp.concatenate(outs, axis=-1)


def _in_proj_kernel(x_ref, g_ref, w_ref, wdt_ref, qg_ref, kg_ref, cos_ref, sin_ref, bsel_ref,
                    q_ref, kf_ref, kb_ref, vf_ref, vb_ref, z_ref, xbc_ref, dt_ref, ga_ref, gb_ref,
                    h_ref, *, transpose_k):
    j = pl.program_id(1)

    @pl.when(j == 0)
    def _():
        x = x_ref[...]
        ms = jnp.mean(x * x, axis=-1, keepdims=True)
        h = x * lax.rsqrt(ms + EPS) * g_ref[...]
        h_ref[...] = h.astype(BF16)
        dt_ref[...] = _dot(h_ref[...], wdt_ref[...])

    acc = _dot(h_ref[...], w_ref[...])

    @pl.when((j >= J_Q) & (j < J_K))
    def _():
        r = _rope_norm(acc, qg_ref[...], cos_ref[...], sin_ref[...], bsel_ref[...])
        q_ref[...] = r.astype(BF16)

    @pl.when((j >= J_K) & (j < J_V))
    def _():
        r = _rope_norm(acc, kg_ref[...], cos_ref[...], sin_ref[...], bsel_ref[...])
        if transpose_k:
            rt = r.T
            kf_ref[0] = rt
            kb_ref[0] = rt.astype(BF16)
        else:
            kf_ref[...] = r
            kb_ref[...] = r.astype(BF16)

    @pl.when((j >= J_V) & (j < J_Z))
    def _():
        vf_ref[...] = acc
        vb_ref[...] = acc.astype(BF16)

    @pl.when((j >= J_Z) & (j < J_X))
    def _():
        z_ref[...] = acc

    @pl.when((j >= J_X) & (j < J_GA))
    def _():
        xbc_ref[...] = acc

    @pl.when((j >= J_GA) & (j < J_GB))
    def _():
        ga_ref[...] = acc

    @pl.when(j >= J_GB)
    def _():
        gb_ref[...] = acc


def _in_proj(x, g, w_main, w_dt, qg, kg, cos_t, sin_t, bsel, tm, seq_len):
    n = x.shape[0]
    rope_blocks = cos_t.shape[0] // tm
    grid = (n // tm, N_PROJ_TILES)
    row = lambda i, j: (i, 0)
    const = lambda i, j: (0, 0)
    rope_map = lambda i, j: (i % rope_blocks, 0)
    in_specs = [
        pl.BlockSpec((tm, D_MODEL), row),
        pl.BlockSpec((1, D_MODEL), const),
        pl.BlockSpec((D_MODEL, PROJ_TILE), lambda i, j: (0, j)),
        pl.BlockSpec((D_MODEL, LANES), const),
        pl.BlockSpec((1, LANES), const),
        pl.BlockSpec((1, LANES), const),
        pl.BlockSpec((tm, LANES), rope_map),
        pl.BlockSpec((tm, LANES), rope_map),
        pl.BlockSpec((LANES, LANES), const),
    ]
    def section(start, stop):
        return lambda i, j: (i, jnp.clip(j - start, 0, stop - start - 1))

    transpose_k = seq_len is not None
    if transpose_k:
        tiles_per_seq = seq_len // tm
        k_shape = (n // seq_len, ATT_WIDTH, seq_len)
        k_spec = pl.BlockSpec(
            (1, PROJ_TILE, tm),
            lambda i, j: (i // tiles_per_seq, jnp.clip(j - J_K, 0, J_V - J_K - 1), i % tiles_per_seq))
    else:
        k_shape = (n, ATT_WIDTH)
        k_spec = pl.BlockSpec((tm, PROJ_TILE), section(J_K, J_V))
    out_shape = [
        jax.ShapeDtypeStruct((n, ATT_WIDTH), BF16),
        jax.ShapeDtypeStruct(k_shape, F32),
        jax.ShapeDtypeStruct(k_shape, BF16),
        jax.ShapeDtypeStruct((n, ATT_WIDTH), F32),
        jax.ShapeDtypeStruct((n, ATT_WIDTH), BF16),
        jax.ShapeDtypeStruct((n, D_INNER), F32),
        jax.ShapeDtypeStruct((n, CONV_DIM), F32),
        jax.ShapeDtypeStruct((n, LANES), F32),
        jax.ShapeDtypeStruct((n, D_MODEL), F32),
        jax.ShapeDtypeStruct((n, D_MODEL), F32),
    ]
    out_specs = [
        pl.BlockSpec((tm, PROJ_TILE), section(J_Q, J_K)),
        k_spec,
        k_spec,
        pl.BlockSpec((tm, PROJ_TILE), section(J_V, J_Z)),
        pl.BlockSpec((tm, PROJ_TILE), section(J_V, J_Z)),
        pl.BlockSpec((tm, PROJ_TILE), section(J_Z, J_X)),
        pl.BlockSpec((tm, PROJ_TILE), section(J_X, J_GA)),
        pl.BlockSpec((tm, LANES), row),
        pl.BlockSpec((tm, PROJ_TILE), section(J_GA, J_GB)),
        pl.BlockSpec((tm, PROJ_TILE), section(J_GB, N_PROJ_TILES)),
    ]
    return pl.pallas_call(
        functools.partial(_in_proj_kernel, transpose_k=transpose_k),
        out_shape=out_shape,
        grid=grid,
        in_specs=in_specs,
        out_specs=out_specs,
        scratch_shapes=[pltpu.VMEM((tm, D_MODEL), BF16)],
        compiler_params=_cparams(("parallel", "arbitrary")),
        name="in_proj",
    )(x, g, w_main, w_dt, qg, kg, cos_t, sin_t, bsel)


def _subln(o, g, out_scale):
    ms = jnp.mean(o * o, axis=-1, keepdims=True)
    return o * lax.rsqrt(ms + EPS) * g * out_scale


def _prompt_attn_kernel(lam_ref, q_ref, kt_ref, v_ref, g_ref, o_ref, *, blk, out_scale):
    t = q_ref.shape[1]
    lam = lam_ref[0, 0]
    row = lax.broadcasted_iota(jnp.int32, (blk, blk), 0)
    col = lax.broadcasted_iota(jnp.int32, (blk, blk), 1)
    causal = col <= row
    for qi in range(t // blk):
        r0 = qi * blk
        q = q_ref[0, r0:r0 + blk, :]
        v_diag = v_ref[0, r0:r0 + blk, :]
        os = []
        for c in range(2):
            cs = slice(c * HEAD_DIM, (c + 1) * HEAD_DIM)
            qc = q[:, cs]
            s_d = jnp.where(causal, _dot(qc, kt_ref[0, cs, r0:r0 + blk]), NEG_BIG)
            m = jnp.max(s_d, axis=-1, keepdims=True)
            if qi > 0:
                s_o = _dot(qc, kt_ref[0, cs, 0:r0])
                m = jnp.maximum(m, jnp.max(s_o, axis=-1, keepdims=True))
                p_o = _exp2(s_o - m)
                l = jnp.sum(p_o, axis=-1, keepdims=True)
                acc = _dot(p_o.astype(BF16), v_ref[0, 0:r0, :])
            p_d = _exp2(s_d - m)
            if qi > 0:
                l = l + jnp.sum(p_d, axis=-1, keepdims=True)
                acc = acc + _dot(p_d.astype(BF16), v_diag)
            else:
                l = jnp.sum(p_d, axis=-1, keepdims=True)
                acc = _dot(p_d.astype(BF16), v_diag)
            os.append(acc / l)
        o = os[0] - lam * os[1]
        o_ref[0, r0:r0 + blk, :] = _subln(o, g_ref[...], out_scale).astype(o_ref.dtype)


def _prompt_attention(q, kt, v, lam, subln_g, out_scale, blk):
    b, t, _ = q.shape
    kern = functools.partial(_prompt_attn_kernel, blk=blk, out_scale=out_scale)
    return pl.pallas_call(
        kern,
        out_shape=jax.ShapeDtypeStruct((b, t, ATT_WIDTH), BF16),
        grid=(b, N_HEADS),
        in_specs=[
            pl.BlockSpec(memory_space=pltpu.SMEM),
            pl.BlockSpec((1, t, V_DIM), lambda bi, h: (bi, 0, h)),
            pl.BlockSpec((1, V_DIM, t), lambda bi, h: (bi, h, 0)),
            pl.BlockSpec((1, t, V_DIM), lambda bi, h: (bi, 0, h)),
            pl.BlockSpec((1, V_DIM), lambda bi, h: (0, 0)),
        ],
        out_specs=pl.BlockSpec((1, t, V_DIM), lambda bi, h: (bi, 0, h)),
        compiler_params=_cparams(("parallel", "parallel")),
        name="prompt_attention",
    )(lam, q, kt, v, subln_g)


def _sample_attn_kernel(pt_ref, lam_ref, wq_ref, knew_ref, vnew_ref, g_ref, *rest,
                        n_valid, out_scale):
    kp_refs = rest[:PAGES_PER_STEP]
    vp_refs = rest[PAGES_PER_STEP:2 * PAGES_PER_STEP]
    o_ref = rest[2 * PAGES_PER_STEP]
    m_ref, l_ref, acc_ref = rest[2 * PAGES_PER_STEP + 1:]
    j = pl.program_id(1)

    @pl.when(j == 0)
    def _():
        m_ref[...] = jnp.full(m_ref.shape, NEG_BIG, F32)
        l_ref[...] = jnp.zeros(l_ref.shape, F32)
        acc_ref[...] = jnp.zeros(acc_ref.shape, F32)

    s = jnp.concatenate(
        [jnp.concatenate([_dot(wq_ref[0, h], kp_refs[p][0, 0, h]) for p in range(PAGES_PER_STEP)],
                         axis=-1) for h in range(N_HEADS)], axis=0)
    m = m_ref[...]
    m_new = jnp.maximum(m, jnp.max(s, axis=-1, keepdims=True))
    alpha = _exp2(m - m_new)
    pr = _exp2(s - m_new)
    l_ref[...] = alpha * l_ref[...] + jnp.sum(pr, axis=-1, keepdims=True)
    m_ref[...] = m_new
    pvs = []
    for h in range(N_HEADS):
        ph = pr[h * QROWS:(h + 1) * QROWS]
        v_rows = pl.ds(h, PAGE_SIZE, stride=N_HEADS)
        pv = _dot(ph[:, :PAGE_SIZE], vp_refs[0][0, 0, v_rows, :])
        for p in range(1, PAGES_PER_STEP):
            pv = pv + _dot(ph[:, p * PAGE_SIZE:(p + 1) * PAGE_SIZE], vp_refs[p][0, 0, v_rows, :])
        pvs.append(pv)
    acc_ref[...] = alpha * acc_ref[...] + jnp.concatenate(pvs, axis=0)

    @pl.when(j == pl.num_programs(1) - 1)
    def _():
        lam = lam_ref[0, 0]
        row_t = lax.broadcasted_iota(jnp.int32, (QROWS, DEC_ROWS), 0) % DEC_ROWS
        col_t = lax.broadcasted_iota(jnp.int32, (QROWS, DEC_ROWS), 1)
        valid = (col_t <= row_t) & (col_t < n_valid)
        outs = []
        for h in range(N_HEADS):
            rs = slice(h * QROWS, (h + 1) * QROWS)
            hs = slice(h * V_DIM, (h + 1) * V_DIM)
            sn = jnp.where(valid, _dot_nt(wq_ref[0, h], knew_ref[0][:, hs]), NEG_BIG)
            mm = m_ref[rs, :]
            mf = jnp.maximum(mm, jnp.max(sn, axis=-1, keepdims=True))
            al = _exp2(mm - mf)
            pn = _exp2(sn - mf)
            lf = al * l_ref[rs, :] + jnp.sum(pn, axis=-1, keepdims=True)
            af = al * acc_ref[rs, :] + _dot(pn, vnew_ref[0][:, hs])
            of = af / lf
            od = of[:DEC_ROWS] - lam * of[DEC_ROWS:]
            outs.append(_subln(od, g_ref[...], out_scale))
        o_ref[0] = jnp.concatenate(outs, axis=-1)


def _sample_attention(page_table, lam, wq, k_new, v_new, subln_g, cache_kt, cache_v, layer,
                      n_valid, out_scale):
    db, n_pages = page_table.shape
    n_steps = n_pages // PAGES_PER_STEP

    def page_map(p, ndim):
        return lambda b, j, pt: (layer, pt[b * n_pages + j * PAGES_PER_STEP + p]) + (0,) * (ndim - 2)

    kern = functools.partial(_sample_attn_kernel, n_valid=n_valid, out_scale=out_scale)
    grid_spec = pltpu.PrefetchScalarGridSpec(
        num_scalar_prefetch=1,
        grid=(db, n_steps),
        in_specs=[
            pl.BlockSpec(memory_space=pltpu.SMEM),
            pl.BlockSpec((1, N_HEADS, QROWS, V_DIM), lambda b, j, pt: (b, 0, 0, 0)),
            pl.BlockSpec((1, DEC_ROWS, ATT_WIDTH), lambda b, j, pt: (b, 0, 0)),
            pl.BlockSpec((1, DEC_ROWS, ATT_WIDTH), lambda b, j, pt: (b, 0, 0)),
            pl.BlockSpec((1, V_DIM), lambda b, j, pt: (0, 0)),
        ] + [pl.BlockSpec((1, 1, N_HEADS, V_DIM, PAGE_SIZE), page_map(p, 5))
             for p in range(PAGES_PER_STEP)]
          + [pl.BlockSpec((1, 1, PAGE_SIZE * N_HEADS, V_DIM), page_map(p, 4))
             for p in range(PAGES_PER_STEP)],
        out_specs=pl.BlockSpec((1, DEC_ROWS, ATT_WIDTH), lambda b, j, pt: (b, 0, 0)),
        scratch_shapes=[
            pltpu.VMEM((N_HEADS * QROWS, 1), F32),
            pltpu.VMEM((N_HEADS * QROWS, 1), F32),
            pltpu.VMEM((N_HEADS * QROWS, V_DIM), F32),
        ],
    )
    return pl.pallas_call(
        kern,
        out_shape=jax.ShapeDtypeStruct((db, DEC_ROWS, ATT_WIDTH), F32),
        grid_spec=grid_spec,
        compiler_params=_cparams(("parallel", "arbitrary")),
        name="sample_attention",
    )(page_table.reshape(-1), lam, wq, k_new, v_new, subln_g,
      *([cache_kt] * PAGES_PER_STEP), *([cache_v] * PAGES_PER_STEP))


def _split3(v):
    hi = v.astype(BF16)
    r1 = v - hi.astype(F32)
    mid = r1.astype(BF16)
    lo = (r1 - mid.astype(F32)).astype(BF16)
    return hi, mid, lo


def _softplus(x):
    return jnp.maximum(x, 0.0) + jnp.log1p(jnp.exp(-jnp.abs(x)))


def _ssd_kernel(*refs, chunk, n_valid, has_init):
    (xbc_ref, z_ref, dt_ref, dtt_ref, cinit_ref) = refs[:5]
    k = 5
    if has_init:
        sinit_ref = refs[k]
        k += 1
    (cw_ref, cb_ref, bias_ref, biast_ref, alog_ref, alogt_ref, dskip_ref, ng_ref,
     ltri_ref, utri_ref, e64_ref, e128_ref) = refs[k:k + 12]
    y_ref, sout_ref = refs[k + 12:k + 14]
    tail_ref, st_ref = refs[k + 14:]
    c = pl.program_id(1)
    L = chunk

    @pl.when(c == 0)
    def _():
        tail_ref[...] = cinit_ref[0]
        if has_init:
            st_ref[...] = sinit_ref[0].T
        else:
            st_ref[...] = jnp.zeros(st_ref.shape, F32)

    xraw = xbc_ref[0]
    head = jnp.concatenate([tail_ref[...], xraw[:SUBLANES]], axis=0)
    conv = cb_ref[...]
    conv_head = cb_ref[...]
    for kk in range(CONV_W):
        back = CONV_W - 1 - kk
        wk = cw_ref[kk:kk + 1, :]
        shifted = pltpu.roll(xraw, back, 0) if back else xraw
        conv = conv + shifted * wk
        conv_head = conv_head + head[SUBLANES - back:2 * SUBLANES - back] * wk
    conv = jnp.concatenate([conv_head, conv[SUBLANES:]], axis=0)
    tail_ref[...] = xraw[L - SUBLANES:, :]
    xact = conv * jax.nn.sigmoid(conv)
    xs = xact[:, :D_INNER]

    nh = N_SSM_HEADS
    dt = _softplus(dt_ref[0][:, :nh] + bias_ref[...])
    dtt = _softplus(dtt_ref[0] + biast_ref[...])
    if n_valid is not None:
        tok = c * L + lax.broadcasted_iota(jnp.int32, (L, 1), 0)
        dt = jnp.where(tok < n_valid, dt, 0.0)
        tokt = c * L + lax.broadcasted_iota(jnp.int32, (1, L), 1)
        dtt = jnp.where(tokt < n_valid, dtt, 0.0)
    da = dt * (-jnp.exp(alog_ref[...]))
    dat = dtt * (-jnp.exp(alogt_ref[...]))
    ltri = ltri_ref[...]
    utri = utri_ref[...]
    a = sum(_dot(ltri, part) for part in _split3(da))
    at = sum(_dot(part, utri) for part in _split3(dat))
    a_last = a[L - 1:L, :]
    w = jnp.exp(a_last - a) * dt

    def expand(v, e_ref):
        return _dot(jnp.concatenate(_split3(v), axis=-1), e_ref[...])

    a128 = expand(a, e128_ref)
    a64 = expand(a, e64_ref)
    w64 = expand(w, e64_ref)
    ea64 = jnp.exp(a64)
    xw = (xs * w64).astype(BF16)

    row = lax.broadcasted_iota(jnp.int32, (L, L), 0)
    col = lax.broadcasted_iota(jnp.int32, (L, L), 1)
    causal = col <= row
    lane = lax.broadcasted_iota(jnp.int32, (1, LANES), 1)
    low = lane < SSM_HEAD_DIM

    ys = []
    for g in range(N_GROUPS):
        bg = xact[:, D_INNER + g * D_STATE:D_INNER + (g + 1) * D_STATE]
        cg = xact[:, D_INNER + (N_GROUPS + g) * D_STATE:D_INNER + (N_GROUPS + g + 1) * D_STATE]
        bgb = bg.astype(BF16)
        cgb = cg.astype(BF16)
        gmat = _dot_nt(cgb, bgb)
        gs = slice(g * GROUP_WIDTH, (g + 1) * GROUP_WIDTH)
        st_g = st_ref[:, gs]
        y_off = _dot(cgb, st_g.astype(BF16)) * ea64[:, gs]
        y_diag = []
        for pr in range(HEADS_PER_GROUP // 2):
            ms = []
            for hh in range(2):
                h = g * HEADS_PER_GROUP + 2 * pr + hh
                seg = a128[:, h * LANES:(h + 1) * LANES] - at[h:h + 1, :]
                lm = jnp.where(causal, jnp.exp(seg), 0.0)
                ms.append((gmat * lm * dtt[h:h + 1, :]).astype(BF16))
            h0 = g * HEADS_PER_GROUP + 2 * pr
            xpair = xs[:, h0 * SSM_HEAD_DIM:(h0 + 2) * SSM_HEAD_DIM]
            xstack = jnp.concatenate([jnp.where(low, xpair, 0.0), jnp.where(low, 0.0, xpair)],
                                     axis=0).astype(BF16)
            y_diag.append(_dot(jnp.concatenate(ms, axis=-1), xstack))
        ys.append(jnp.concatenate(y_diag, axis=-1) + y_off)
        cd = ea64[L - 1:L, gs]
        st_ref[:, gs] = cd * st_g + _dot(bg.T.astype(BF16), xw[:, gs])
    y = jnp.concatenate(ys, axis=-1) + dskip_ref[...] * xs
    zz = z_ref[0]
    y = y * (zz * jax.nn.sigmoid(zz))
    outs = []
    for g in range(N_GROUPS):
        yg = y[:, g * GROUP_WIDTH:(g + 1) * GROUP_WIDTH]
        ms = jnp.mean(yg * yg, axis=-1, keepdims=True)
        outs.append(yg * lax.rsqrt(ms + EPS) * ng_ref[:, g * GROUP_WIDTH:(g + 1) * GROUP_WIDTH])
    y_ref[0] = jnp.concatenate(outs, axis=-1).astype(y_ref.dtype)

    @pl.when(c == pl.num_programs(1) - 1)
    def _():
        sout_ref[0] = st_ref[...].T


def _ssd_constants(chunk):
    idx = np.arange(chunk)
    ltri = (idx[None, :] <= idx[:, None]).astype(np.float32)
    utri = ltri.T
    k3 = np.arange(3 * N_SSM_HEADS) % N_SSM_HEADS
    e64 = (k3[:, None] == (np.arange(D_INNER)[None, :] // SSM_HEAD_DIM)).astype(np.float32)
    e128 = (k3[:, None] == (np.arange(N_SSM_HEADS * LANES)[None, :] // LANES)).astype(np.float32)
    return tuple(jnp.asarray(m, dtype=BF16) for m in (ltri, utri, e64, e128))


def _ssd(xbc, z, dt_raw, conv_init, ssm_init, p, n_valid):
    b, t, _ = xbc.shape
    chunk = SSD_CHUNK
    nc = t // chunk
    has_init = ssm_init is not None
    dtt = jnp.swapaxes(dt_raw[:, :, :N_SSM_HEADS], 1, 2)
    cinit = jnp.zeros((b, SUBLANES, CONV_DIM), F32)
    cinit = cinit.at[:, SUBLANES - (CONV_W - 1):, :].set(conv_init)
    consts = _ssd_constants(chunk)
    bmap = lambda bi, c: (bi, c, 0)
    const2 = lambda bi, c: (0, 0)
    args = [xbc, z, dt_raw, dtt, cinit]
    in_specs = [
        pl.BlockSpec((1, chunk, CONV_DIM), bmap),
        pl.BlockSpec((1, chunk, D_INNER), bmap),
        pl.BlockSpec((1, chunk, LANES), bmap),
        pl.BlockSpec((1, N_SSM_HEADS, chunk), lambda bi, c: (bi, 0, c)),
        pl.BlockSpec((1, SUBLANES, CONV_DIM), lambda bi, c: (bi, 0, 0)),
    ]
    if has_init:
        args.append(ssm_init.reshape(b, D_INNER, D_STATE))
        in_specs.append(pl.BlockSpec((1, D_INNER, D_STATE), lambda bi, c: (bi, 0, 0)))
    params = [p['conv_w'], p['conv_b'][None, :], p['dt_bias'][None, :], p['dt_bias'][:, None],
              p['a_log'][None, :], p['a_log'][:, None],
              jnp.repeat(p['d_skip'], SSM_HEAD_DIM)[None, :], p['ssm_norm_g'][None, :]]
    args += params + list(consts)
    in_specs += [pl.BlockSpec(a.shape, const2) for a in params + list(consts)]
    kern = functools.partial(_ssd_kernel, chunk=chunk, n_valid=n_valid, has_init=has_init)
    y, s_out = pl.pallas_call(
        kern,
        out_shape=[jax.ShapeDtypeStruct((b, t, D_INNER), BF16),
                   jax.ShapeDtypeStruct((b, D_INNER, D_STATE), F32)],
        grid=(b, nc),
        in_specs=in_specs,
        out_specs=[pl.BlockSpec((1, chunk, D_INNER), bmap),
                   pl.BlockSpec((1, D_INNER, D_STATE), lambda bi, c: (bi, 0, 0))],
        scratch_shapes=[pltpu.VMEM((SUBLANES, CONV_DIM), F32),
                        pltpu.VMEM((D_STATE, D_INNER), F32)],
        compiler_params=_cparams(("parallel", "arbitrary")),
        name="ssd",
    )(*args)
    return y, s_out.reshape(b, N_SSM_HEADS, SSM_HEAD_DIM, D_STATE)


def _merge_kernel(o_ref, y_ref, ga_ref, gb_ref, x_ref, wa_ref, wb_ref, wo_ref, out_ref):
    ba = _dot(o_ref[...], wa_ref[...])
    bb = _dot(y_ref[...], wb_ref[...])
    mix = jax.nn.sigmoid(ga_ref[...]) * ba + jax.nn.sigmoid(gb_ref[...]) * bb
    out_ref[...] = x_ref[...] + _dot(mix.astype(BF16), wo_ref[...])


def _merge(o, y, ga, gb, x, wa, wb, wo, tm):
    n = x.shape[0]
    row = lambda i: (i, 0)
    const = lambda i: (0, 0)
    return pl.pallas_call(
        _merge_kernel,
        out_shape=jax.ShapeDtypeStruct((n, D_MODEL), F32),
        grid=(n // tm,),
        in_specs=[
            pl.BlockSpec((tm, ATT_WIDTH), row),
            pl.BlockSpec((tm, D_INNER), row),
            pl.BlockSpec((tm, D_MODEL), row),
            pl.BlockSpec((tm, D_MODEL), row),
            pl.BlockSpec((tm, D_MODEL), row),
            pl.BlockSpec((ATT_WIDTH, D_MODEL), const),
            pl.BlockSpec((D_INNER, D_MODEL), const),
            pl.BlockSpec((D_MODEL, D_MODEL), const),
        ],
        out_specs=pl.BlockSpec((tm, D_MODEL), row),
        compiler_params=_cparams(("parallel",)),
        name="merge",
    )(o, y, ga, gb, x, wa, wb, wo)


def _mlp_kernel(x_ref, g_ref, wu_ref, wd_ref, out_ref, h_ref, acc_ref):
    k = pl.program_id(1)

    @pl.when(k == 0)
    def _():
        x = x_ref[...]
        ms = jnp.mean(x * x, axis=-1, keepdims=True)
        h_ref[...] = (x * lax.rsqrt(ms + EPS) * g_ref[...]).astype(BF16)
        acc_ref[...] = x

    u = _dot(h_ref[...], wu_ref[...])
    r = jnp.square(jnp.maximum(u, 0.0))
    acc_ref[...] += _dot(r.astype(BF16), wd_ref[...])

    @pl.when(k == pl.num_programs(1) - 1)
    def _():
        out_ref[...] = acc_ref[...]


def _mlp(x, g, wu, wd, tm, tk):
    n = x.shape[0]
    return pl.pallas_call(
        _mlp_kernel,
        out_shape=jax.ShapeDtypeStruct((n, D_MODEL), F32),
        grid=(n // tm, D_FF // tk),
        in_specs=[
            pl.BlockSpec((tm, D_MODEL), lambda i, k: (i, 0)),
            pl.BlockSpec((1, D_MODEL), lambda i, k: (0, 0)),
            pl.BlockSpec((D_MODEL, tk), lambda i, k: (0, k)),
            pl.BlockSpec((tk, D_MODEL), lambda i, k: (k, 0)),
        ],
        out_specs=pl.BlockSpec((tm, D_MODEL), lambda i, k: (i, 0)),
        scratch_shapes=[pltpu.VMEM((tm, D_MODEL), BF16), pltpu.VMEM((tm, D_MODEL), F32)],
        compiler_params=_cparams(("parallel", "arbitrary")),
        name="mlp",
    )(x, g, wu, wd)


def _rope_tables(pos):
    half = HEAD_DIM // 2
    inv = ROPE_THETA ** (-jnp.arange(half, dtype=F32) / half)
    ang = pos.astype(F32)[:, None] * inv[None, :]
    cos, sin = jnp.cos(ang), jnp.sin(ang)
    cos128 = jnp.tile(cos, (1, LANES // half))
    sin128 = jnp.tile(jnp.concatenate([-sin, sin], axis=-1), (1, LANES // HEAD_DIM))
    return cos128, sin128


def _pack_weights(p):
    sizes = (ATT_WIDTH, ATT_WIDTH, ATT_WIDTH, D_INNER, CONV_DIM, N_SSM_HEADS, D_MODEL, D_MODEL)
    offs = np.concatenate([[0], np.cumsum(sizes)])
    sec = [p['w_in'][:, offs[i]:offs[i + 1]] for i in range(len(sizes))]
    w_main = jnp.concatenate(sec[:5] + sec[6:], axis=1).astype(BF16)
    w_dt = jnp.pad(sec[5], ((0, 0), (0, LANES - N_SSM_HEADS))).astype(BF16)
    return dict(
        w_main=w_main, w_dt=w_dt,
        wa=p['w_branch_a'].astype(BF16), wb=p['w_branch_b'].astype(BF16),
        wo=p['w_out'].astype(BF16), wu=p['w_up'].astype(BF16), wd=p['w_down'].astype(BF16),
    )


def _mixer_common(x2d, pos_tables, p, w, tm, seq_len):
    idx = np.arange(LANES)
    bsel = jnp.asarray((idx[:, None] // HEAD_DIM == idx[None, :] // HEAD_DIM).astype(np.float32),
                       dtype=BF16)
    q_scale = (HEAD_DIM ** -0.5) * LOG2E
    qg = (jnp.tile(p['q_norm_g'], LANES // HEAD_DIM) * q_scale)[None, :]
    kg = jnp.tile(p['k_norm_g'], LANES // HEAD_DIM)[None, :]
    cos_t, sin_t = pos_tables
    return _in_proj(x2d, p['norm_mix_g'][None, :], w['w_main'], w['w_dt'], qg, kg, cos_t, sin_t,
                    bsel, tm, seq_len)


def _finish(o, y_ssm, ga, gb, x2d, p, w, tm):
    x1 = _merge(o, y_ssm, ga, gb, x2d, w['wa'], w['wb'], w['wo'], tm)
    return _mlp(x1, p['norm_mlp_g'][None, :], w['wu'], w['wd'], tm, 1024)


def _lam(p, lam_init):
    lam = (jnp.exp(jnp.sum(p['lambda_q1'].astype(F32) * p['lambda_k1'].astype(F32)))
           - jnp.exp(jnp.sum(p['lambda_q2'].astype(F32) * p['lambda_k2'].astype(F32))) + lam_init)
    return lam.reshape(1, 1).astype(F32)


def _prompt_layer(x, p, w, lam, lam_init):
    b, t, _ = x.shape
    n = b * t
    tm = min(512, t)
    x2d = x.reshape(n, D_MODEL)
    tables = _rope_tables(jnp.arange(t))
    q, ktf, ktb, vf, vb, z, xbc, dt, ga, gb = _mixer_common(x2d, tables, p, w, min(1024, t), t)
    blk = min(512, t)
    o = _prompt_attention(q.reshape(b, t, -1), ktb, vb.reshape(b, t, -1), lam,
                          p['subln_g'][None, :], 1.0 - lam_init, blk)
    xbc3 = xbc.reshape(b, t, CONV_DIM)
    conv0 = jnp.zeros((b, CONV_W - 1, CONV_DIM), F32)
    y_ssm, new_ssm = _ssd(xbc3, z.reshape(b, t, D_INNER), dt.reshape(b, t, LANES), conv0, None, p,
                          None)
    y = _finish(o.reshape(n, -1), y_ssm.reshape(n, -1), ga, gb, x2d, p, w, tm)
    new_conv = jnp.concatenate([conv0, xbc3], axis=1)[:, -(CONV_W - 1):]
    k_out = jnp.transpose(ktf.reshape(b, N_HEADS, 2, HEAD_DIM, t), (0, 4, 1, 2, 3))
    return (y.reshape(b, t, D_MODEL), k_out, vf.reshape(b, t, N_HEADS, V_DIM), new_conv, new_ssm)


def _sample_layer(x, p, w, lam, lam_init, cache_k, cache_v, page_table, layer, conv_state,
                  ssm_state):
    b, t, _ = x.shape
    assert t <= DEC_ROWS
    n = b * t
    tm = n
    past = page_table.shape[1] * PAGE_SIZE
    x2d = x.reshape(n, D_MODEL)
    cos_t, sin_t = _rope_tables(past + jnp.arange(t))
    tables = (jnp.tile(cos_t, (b, 1)), jnp.tile(sin_t, (b, 1)))
    q, kf, kb, vf, vb, z, xbc, dt, ga, gb = _mixer_common(x2d, tables, p, w, tm, None)

    q5 = q.reshape(b, t, N_HEADS, 2, HEAD_DIM)
    q5 = jnp.pad(q5, ((0, 0), (0, DEC_ROWS - t), (0, 0), (0, 0), (0, 0)))
    eye_c = jnp.eye(2, dtype=BF16)
    wq = jnp.einsum('bthcd,cC->bhctCd', q5, eye_c).reshape(b, N_HEADS, QROWS, V_DIM).astype(F32)
    pad_t = lambda a: jnp.pad(a.reshape(b, t, -1), ((0, 0), (0, DEC_ROWS - t), (0, 0)))
    ckt = jnp.transpose(cache_k, (0, 1, 3, 4, 5, 2)).reshape(
        cache_k.shape[0], cache_k.shape[1], N_HEADS, V_DIM, PAGE_SIZE)
    cv = cache_v.reshape(cache_v.shape[0], cache_v.shape[1], PAGE_SIZE * N_HEADS, V_DIM)
    o = _sample_attention(page_table, lam, wq, pad_t(kf), pad_t(vf), p['subln_g'][None, :], ckt,
                          cv, layer, t, 1.0 - lam_init)
    o = o[:, :t].reshape(n, ATT_WIDTH).astype(BF16)

    tpad = SSD_CHUNK
    pad_c = lambda a: jnp.pad(a.reshape(b, t, -1), ((0, 0), (0, tpad - t), (0, 0)))
    xbc3 = xbc.reshape(b, t, CONV_DIM)
    y_ssm, new_ssm = _ssd(pad_c(xbc), pad_c(z), pad_c(dt), conv_state, ssm_state, p, t)
    y_ssm = y_ssm[:, :t].reshape(n, D_INNER)
    y = _finish(o, y_ssm, ga, gb, x2d, p, w, tm)
    new_conv = jnp.concatenate([conv_state, xbc3], axis=1)[:, -(CONV_W - 1):]
    return (y.reshape(b, t, D_MODEL), kf.reshape(b, t, N_HEADS, 2, HEAD_DIM),
            vf.reshape(b, t, N_HEADS, V_DIM), new_conv, new_ssm)


def kernel(x_prompt, x_sample, cache_k, cache_v, state_conv, state_ssm, page_table, norm_mix_g, w_in, q_norm_g, k_norm_g, lambda_q1, lambda_k1, lambda_q2, lambda_k2, subln_g, conv_w, conv_b, dt_bias, a_log, d_skip, ssm_norm_g, w_branch_a, w_branch_b, w_out, norm_mlp_g, w_up, w_down):
    depth = w_in.shape[0]
    yp, ys = x_prompt, x_sample
    outs = [[] for _ in range(8)]
    for l in range(depth):
        p = {
            'norm_mix_g': norm_mix_g[l], 'w_in': w_in[l], 'q_norm_g': q_norm_g[l], 'k_norm_g': k_norm_g[l],
            'lambda_q1': lambda_q1[l], 'lambda_k1': lambda_k1[l], 'lambda_q2': lambda_q2[l], 'lambda_k2': lambda_k2[l],
            'subln_g': subln_g[l], 'conv_w': conv_w[l], 'conv_b': conv_b[l], 'dt_bias': dt_bias[l],
            'a_log': a_log[l], 'd_skip': d_skip[l], 'ssm_norm_g': ssm_norm_g[l],
            'w_branch_a': w_branch_a[l], 'w_branch_b': w_branch_b[l], 'w_out': w_out[l],
            'norm_mlp_g': norm_mlp_g[l], 'w_up': w_up[l], 'w_down': w_down[l],
        }
        lam_init = 0.8 - 0.6 * math.exp(-0.3 * l)
        lam = _lam(p, lam_init)
        w = _pack_weights(p)
        yp, kp, vp, cp, sp = _prompt_layer(yp, p, w, lam, lam_init)
        ys, ks, vs, cs, ss = _sample_layer(ys, p, w, lam, lam_init, cache_k, cache_v, page_table, l,
                                           state_conv[l], state_ssm[l])
        for lst, val in zip(outs, (kp, vp, cp, sp, ks, vs, cs, ss)):
            lst.append(val)
    return (yp, ys) + tuple(jnp.stack(o) for o in outs)
```

```python
import functools
import math

import numpy as np
import jax
import jax.numpy as jnp
from jax import lax
from jax.experimental import pallas as pl
from jax.experimental.pallas import tpu as pltpu

F32 = jnp.float32
BF16 = jnp.bfloat16

D_MODEL = 1024
N_HEADS = 8
HEAD_DIM = 64
V_DIM = 2 * HEAD_DIM
ATT_WIDTH = N_HEADS * V_DIM
ROPE_THETA = 10000.0
D_INNER = 2 * D_MODEL
SSM_HEAD_DIM = 64
N_SSM_HEADS = D_INNER // SSM_HEAD_DIM
N_GROUPS = 4
HEADS_PER_GROUP = N_SSM_HEADS // N_GROUPS
GROUP_WIDTH = D_INNER // N_GROUPS
D_STATE = 128
CONV_W = 4
CONV_DIM = D_INNER + 2 * N_GROUPS * D_STATE
SSD_CHUNK = 128
D_FF = 4 * D_MODEL
PAGE_SIZE = 128
EPS = 1e-6
LOG2E = 1.4426950408889634

LANES = 128
SUBLANES = 8
VMEM_LIMIT_BYTES = 56 * 1024 * 1024

NEG_BIG = -1e30

_SECTION_WIDTHS = (ATT_WIDTH, ATT_WIDTH, ATT_WIDTH, D_INNER, CONV_DIM, D_MODEL, D_MODEL)


def _section_starts(tn):
    return tuple(int(s) // tn for s in np.cumsum((0,) + _SECTION_WIDTHS))

PAGES_PER_STEP = 8
DEC_ROWS = 8
QROWS = 2 * DEC_ROWS


def _cparams(sem):
    return pltpu.CompilerParams(dimension_semantics=sem, vmem_limit_bytes=VMEM_LIMIT_BYTES)


def _dot(a, b):
    return jnp.dot(a, b, preferred_element_type=F32)


def _dot_nt(a, b):
    return lax.dot_general(a, b, (((1,), (1,)), ((), ())), preferred_element_type=F32)


def _exp2(x):
    return jnp.exp2(x)


def _rope_norm(acc, gain, cos, sin, bsel):
    lane = lax.broadcasted_iota(jnp.int32, (1, LANES), 1)
    first_half = (lane % HEAD_DIM) < (HEAD_DIM // 2)
    outs = []
    for hb in range(acc.shape[1] // LANES):
        xh = acc[:, hb * LANES:(hb + 1) * LANES]
        ss = _dot((xh * xh).astype(BF16), bsel)
        y = xh * lax.rsqrt(ss * (1.0 / HEAD_DIM) + EPS) * gain
        partner = jnp.where(first_half,
                            pltpu.roll(y, LANES - HEAD_DIM // 2, 1),
                            pltpu.roll(y, HEAD_DIM // 2, 1))
        outs.append(y * cos + partner * sin)
    return jnp.concatenate(outs, axis=-1)


def _in_proj_kernel(x_ref, g_ref, w_ref, wdt_ref, qg_ref, kg_ref, cos_ref, sin_ref, bsel_ref,
                    q_ref, k_ref, v_ref, z_ref, xbc_ref, xtail_ref, dt_ref, ga_ref, gb_ref,
                    h_ref, *, starts, transpose_k):
    J_Q, J_K, J_V, J_Z, J_X, J_GA, J_GB, _ = starts
    j = pl.program_id(1)

    @pl.when(j == 0)
    def _():
        x = x_ref[...]
        ms = jnp.mean(x * x, axis=-1, keepdims=True)
        h = x * lax.rsqrt(ms + EPS) * g_ref[...]
        h_ref[...] = h.astype(BF16)
        dt_ref[...] = _dot(h_ref[...], wdt_ref[...])

    acc = _dot(h_ref[...], w_ref[...])

    @pl.when((j >= J_Q) & (j < J_K))
    def _():
        r = _rope_norm(acc, qg_ref[...], cos_ref[...], sin_ref[...], bsel_ref[...])
        q_ref[...] = r.astype(BF16)

    @pl.when((j >= J_K) & (j < J_V))
    def _():
        r = _rope_norm(acc, kg_ref[...], cos_ref[...], sin_ref[...], bsel_ref[...])
        if transpose_k:
            k_ref[0] = r.T
        else:
            k_ref[...] = r

    @pl.when((j >= J_V) & (j < J_Z))
    def _():
        v_ref[...] = acc

    @pl.when((j >= J_Z) & (j < J_X))
    def _():
        z_ref[...] = acc.astype(z_ref.dtype)

    @pl.when((j >= J_X) & (j < J_GA))
    def _():
        xbc_ref[...] = acc.astype(xbc_ref.dtype)
        xtail_ref[0] = acc[acc.shape[0] - SUBLANES:, :]

    @pl.when((j >= J_GA) & (j < J_GB))
    def _():
        ga_ref[...] = acc.astype(ga_ref.dtype)

    @pl.when(j >= J_GB)
    def _():
        gb_ref[...] = acc.astype(gb_ref.dtype)


def _in_proj(x, g, w_main, w_dt, qg, kg, cos_t, sin_t, bsel, tm, tn, seq_len, act_dtype):
    n = x.shape[0]
    rope_blocks = cos_t.shape[0] // tm
    starts = _section_starts(tn)
    J_Q, J_K, J_V, J_Z, J_X, J_GA, J_GB, n_tiles = starts
    grid = (n // tm, n_tiles)
    row = lambda i, j: (i, 0)
    const = lambda i, j: (0, 0)
    rope_map = lambda i, j: (i % rope_blocks, 0)
    in_specs = [
        pl.BlockSpec((tm, D_MODEL), row),
        pl.BlockSpec((1, D_MODEL), const),
        pl.BlockSpec((D_MODEL, tn), lambda i, j: (0, j)),
        pl.BlockSpec((D_MODEL, LANES), const),
        pl.BlockSpec((1, LANES), const),
        pl.BlockSpec((1, LANES), const),
        pl.BlockSpec((tm, LANES), rope_map),
        pl.BlockSpec((tm, LANES), rope_map),
        pl.BlockSpec((LANES, LANES), const),
    ]
    def section(start, stop):
        return lambda i, j: (i, jnp.clip(j - start, 0, stop - start - 1))

    transpose_k = seq_len is not None
    tiles_per_seq = seq_len // tm if transpose_k else 1
    if transpose_k:
        k_shape = (n // seq_len, ATT_WIDTH, seq_len)
        k_spec = pl.BlockSpec(
            (1, tn, tm),
            lambda i, j: (i // tiles_per_seq, jnp.clip(j - J_K, 0, J_V - J_K - 1), i % tiles_per_seq))
    else:
        k_shape = (n, ATT_WIDTH)
        k_spec = pl.BlockSpec((tm, tn), section(J_K, J_V))
    n_seq = n // (tiles_per_seq * tm)
    out_shape = [
        jax.ShapeDtypeStruct((n, ATT_WIDTH), BF16),
        jax.ShapeDtypeStruct(k_shape, F32),
        jax.ShapeDtypeStruct((n, ATT_WIDTH), F32),
        jax.ShapeDtypeStruct((n, D_INNER), act_dtype),
        jax.ShapeDtypeStruct((n, CONV_DIM), act_dtype),
        jax.ShapeDtypeStruct((n_seq, SUBLANES, CONV_DIM), F32),
        jax.ShapeDtypeStruct((n, LANES), F32),
        jax.ShapeDtypeStruct((n, D_MODEL), act_dtype),
        jax.ShapeDtypeStruct((n, D_MODEL), act_dtype),
    ]
    out_specs = [
        pl.BlockSpec((tm, tn), section(J_Q, J_K)),
        k_spec,
        pl.BlockSpec((tm, tn), section(J_V, J_Z)),
        pl.BlockSpec((tm, tn), section(J_Z, J_X)),
        pl.BlockSpec((tm, tn), section(J_X, J_GA)),
        pl.BlockSpec((1, SUBLANES, tn),
                     lambda i, j: (i // tiles_per_seq, 0, jnp.clip(j - J_X, 0, J_GA - J_X - 1))),
        pl.BlockSpec((tm, LANES), row),
        pl.BlockSpec((tm, tn), section(J_GA, J_GB)),
        pl.BlockSpec((tm, tn), section(J_GB, n_tiles)),
    ]
    return pl.pallas_call(
        functools.partial(_in_proj_kernel, starts=starts, transpose_k=transpose_k),
        out_shape=out_shape,
        grid=grid,
        in_specs=in_specs,
        out_specs=out_specs,
        scratch_shapes=[pltpu.VMEM((tm, D_MODEL), BF16)],
        compiler_params=_cparams(("parallel", "arbitrary")),
        name="in_proj",
    )(x, g, w_main, w_dt, qg, kg, cos_t, sin_t, bsel)


def _subln(o, g, out_scale):
    ms = jnp.mean(o * o, axis=-1, keepdims=True)
    return o * lax.rsqrt(ms + EPS) * g * out_scale


def _prompt_attn_kernel(lam_ref, q_ref, kt_ref, v_ref, g_ref, o_ref, *, blk, out_scale):
    t = q_ref.shape[1]
    lam = lam_ref[0, 0]
    row = lax.broadcasted_iota(jnp.int32, (blk, blk), 0)
    col = lax.broadcasted_iota(jnp.int32, (blk, blk), 1)
    causal = col <= row
    kt = kt_ref[0].astype(BF16)
    v = v_ref[0].astype(BF16)
    for qi in range(t // blk):
        r0 = qi * blk
        q = q_ref[0, r0:r0 + blk, :]
        ps = []
        for c in range(2):
            cs = slice(c * HEAD_DIM, (c + 1) * HEAD_DIM)
            qc = q[:, cs]
            s_d = jnp.where(causal, _dot(qc, kt[cs, r0:r0 + blk]), NEG_BIG)
            m = jnp.max(s_d, axis=-1, keepdims=True)
            p_o = None
            if qi > 0:
                s_o = _dot(qc, kt[cs, 0:r0])
                m = jnp.maximum(m, jnp.max(s_o, axis=-1, keepdims=True))
                p_o = _exp2(s_o - m)
            p_d = _exp2(s_d - m)
            l = jnp.sum(p_d, axis=-1, keepdims=True)
            if qi > 0:
                l = l + jnp.sum(p_o, axis=-1, keepdims=True)
            ps.append((p_o, p_d, l))
        w0 = 1.0 / ps[0][2]
        w1 = lam / ps[1][2]
        o = _dot((ps[0][1] * w0 - ps[1][1] * w1).astype(BF16), v[r0:r0 + blk, :])
        if qi > 0:
            o = o + _dot((ps[0][0] * w0 - ps[1][0] * w1).astype(BF16), v[0:r0, :])
        o_ref[0, r0:r0 + blk, :] = _subln(o, g_ref[...], out_scale).astype(o_ref.dtype)


def _prompt_attention(q, kt, v, lam, subln_g, out_scale, blk):
    b, t, _ = q.shape
    kern = functools.partial(_prompt_attn_kernel, blk=blk, out_scale=out_scale)
    return pl.pallas_call(
        kern,
        out_shape=jax.ShapeDtypeStruct((b, t, ATT_WIDTH), BF16),
        grid=(b, N_HEADS),
        in_specs=[
            pl.BlockSpec(memory_space=pltpu.SMEM),
            pl.BlockSpec((1, t, V_DIM), lambda bi, h: (bi, 0, h)),
            pl.BlockSpec((1, V_DIM, t), lambda bi, h: (bi, h, 0)),
            pl.BlockSpec((1, t, V_DIM), lambda bi, h: (bi, 0, h)),
            pl.BlockSpec((1, V_DIM), lambda bi, h: (0, 0)),
        ],
        out_specs=pl.BlockSpec((1, t, V_DIM), lambda bi, h: (bi, 0, h)),
        compiler_params=_cparams(("parallel", "parallel")),
        name="prompt_attention",
    )(lam, q, kt, v, subln_g)


def _sample_attn_kernel(pt_ref, lam_ref, wq_ref, knew_ref, vnew_ref, g_ref, *rest,
                        n_valid, out_scale):
    kp_refs = rest[:PAGES_PER_STEP]
    vp_refs = rest[PAGES_PER_STEP:2 * PAGES_PER_STEP]
    o_ref = rest[2 * PAGES_PER_STEP]
    m_ref, l_ref, acc_ref = rest[2 * PAGES_PER_STEP + 1:]
    j = pl.program_id(1)

    @pl.when(j == 0)
    def _():
        m_ref[...] = jnp.full(m_ref.shape, NEG_BIG, F32)
        l_ref[...] = jnp.zeros(l_ref.shape, F32)
        acc_ref[...] = jnp.zeros(acc_ref.shape, F32)

    s = jnp.concatenate(
        [jnp.concatenate([_dot(wq_ref[0, h], kp_refs[p][0, 0, h]) for p in range(PAGES_PER_STEP)],
                         axis=-1) for h in range(N_HEADS)], axis=0)
    m = m_ref[...]
    m_new = jnp.maximum(m, jnp.max(s, axis=-1, keepdims=True))
    alpha = _exp2(m - m_new)
    pr = _exp2(s - m_new)
    l_ref[...] = alpha * l_ref[...] + jnp.sum(pr, axis=-1, keepdims=True)
    m_ref[...] = m_new
    pvs = []
    for h in range(N_HEADS):
        ph = pr[h * QROWS:(h + 1) * QROWS]
        v_rows = pl.ds(h, PAGE_SIZE, stride=N_HEADS)
        pv = _dot(ph[:, :PAGE_SIZE], vp_refs[0][0, 0, v_rows, :])
        for p in range(1, PAGES_PER_STEP):
            pv = pv + _dot(ph[:, p * PAGE_SIZE:(p + 1) * PAGE_SIZE], vp_refs[p][0, 0, v_rows, :])
        pvs.append(pv)
    acc_ref[...] = alpha * acc_ref[...] + jnp.concatenate(pvs, axis=0)

    @pl.when(j == pl.num_programs(1) - 1)
    def _():
        lam = lam_ref[0, 0]
        row_t = lax.broadcasted_iota(jnp.int32, (QROWS, DEC_ROWS), 0) % DEC_ROWS
        col_t = lax.broadcasted_iota(jnp.int32, (QROWS, DEC_ROWS), 1)
        valid = (col_t <= row_t) & (col_t < n_valid)
        outs = []
        for h in range(N_HEADS):
            rs = slice(h * QROWS, (h + 1) * QROWS)
            hs = slice(h * V_DIM, (h + 1) * V_DIM)
            sn = jnp.where(valid, _dot_nt(wq_ref[0, h], knew_ref[0][:, hs]), NEG_BIG)
            mm = m_ref[rs, :]
            mf = jnp.maximum(mm, jnp.max(sn, axis=-1, keepdims=True))
            al = _exp2(mm - mf)
            pn = _exp2(sn - mf)
            lf = al * l_ref[rs, :] + jnp.sum(pn, axis=-1, keepdims=True)
            af = al * acc_ref[rs, :] + _dot(pn, vnew_ref[0][:, hs])
            of = af / lf
            od = of[:DEC_ROWS] - lam * of[DEC_ROWS:]
            outs.append(_subln(od, g_ref[...], out_scale))
        o_ref[0] = jnp.concatenate(outs, axis=-1)


def _sample_attention(page_table, lam, wq, k_new, v_new, subln_g, cache_kt, cache_v, layer,
                      n_valid, out_scale):
    db, n_pages = page_table.shape
    n_steps = n_pages // PAGES_PER_STEP

    def page_map(p, ndim):
        return lambda b, j, pt: (layer, pt[b * n_pages + j * PAGES_PER_STEP + p]) + (0,) * (ndim - 2)

    kern = functools.partial(_sample_attn_kernel, n_valid=n_valid, out_scale=out_scale)
    grid_spec = pltpu.PrefetchScalarGridSpec(
        num_scalar_prefetch=1,
        grid=(db, n_steps),
        in_specs=[
            pl.BlockSpec(memory_space=pltpu.SMEM),
            pl.BlockSpec((1, N_HEADS, QROWS, V_DIM), lambda b, j, pt: (b, 0, 0, 0)),
            pl.BlockSpec((1, DEC_ROWS, ATT_WIDTH), lambda b, j, pt: (b, 0, 0)),
            pl.BlockSpec((1, DEC_ROWS, ATT_WIDTH), lambda b, j, pt: (b, 0, 0)),
            pl.BlockSpec((1, V_DIM), lambda b, j, pt: (0, 0)),
        ] + [pl.BlockSpec((1, 1, N_HEADS, V_DIM, PAGE_SIZE), page_map(p, 5))
             for p in range(PAGES_PER_STEP)]
          + [pl.BlockSpec((1, 1, PAGE_SIZE * N_HEADS, V_DIM), page_map(p, 4))
             for p in range(PAGES_PER_STEP)],
        out_specs=pl.BlockSpec((1, DEC_ROWS, ATT_WIDTH), lambda b, j, pt: (b, 0, 0)),
        scratch_shapes=[
            pltpu.VMEM((N_HEADS * QROWS, 1), F32),
            pltpu.VMEM((N_HEADS * QROWS, 1), F32),
            pltpu.VMEM((N_HEADS * QROWS, V_DIM), F32),
        ],
    )
    return pl.pallas_call(
        kern,
        out_shape=jax.ShapeDtypeStruct((db, DEC_ROWS, ATT_WIDTH), F32),
        grid_spec=grid_spec,
        compiler_params=_cparams(("parallel", "arbitrary")),
        name="sample_attention",
    )(page_table.reshape(-1), lam, wq, k_new, v_new, subln_g,
      *([cache_kt] * PAGES_PER_STEP), *([cache_v] * PAGES_PER_STEP))


def _split3(v):
    hi = v.astype(BF16)
    r1 = v - hi.astype(F32)
    mid = r1.astype(BF16)
    lo = (r1 - mid.astype(F32)).astype(BF16)
    return hi, mid, lo


def _softplus(x):
    return jnp.maximum(x, 0.0) + jnp.log1p(jnp.exp(-jnp.abs(x)))


def _ssd_kernel(*refs, chunk, n_valid, has_init):
    (xbc_ref, z_ref, dt_ref, dtt_ref, cinit_ref) = refs[:5]
    k = 5
    if has_init:
        sinit_ref = refs[k]
        k += 1
    (cw_ref, cb_ref, bias_ref, biast_ref, alog_ref, alogt_ref, dskip_ref, ng_ref,
     ltri_ref, utri_ref, e64_ref, e128_ref) = refs[k:k + 12]
    y_ref, sout_ref = refs[k + 12:k + 14]
    tail_ref, st_ref = refs[k + 14:]
    c = pl.program_id(1)
    L = chunk

    @pl.when(c == 0)
    def _():
        tail_ref[...] = cinit_ref[0]
        if has_init:
            st_ref[...] = sinit_ref[0].T
        else:
            st_ref[...] = jnp.zeros(st_ref.shape, F32)

    xraw = xbc_ref[0].astype(F32)
    head =jnp.concatenate([tail_ref[...], xraw[:SUBLANES]], axis=0)
    conv = cb_ref[...]
    conv_head = cb_ref[...]
    for kk in range(CONV_W):
        back = CONV_W - 1 - kk
        wk = cw_ref[kk:kk + 1, :]
        shifted = pltpu.roll(xraw, back, 0) if back else xraw
        conv = conv + shifted * wk
        conv_head = conv_head + head[SUBLANES - back:2 * SUBLANES - back] * wk
    conv = jnp.concatenate([conv_head, conv[SUBLANES:]], axis=0)
    tail_ref[...] = xraw[L - SUBLANES:, :]
    xact = conv * jax.nn.sigmoid(conv)
    xs = xact[:, :D_INNER]

    nh = N_SSM_HEADS
    dt = _softplus(dt_ref[0][:, :nh] + bias_ref[...])
    dtt = _softplus(dtt_ref[0] + biast_ref[...])
    if n_valid is not None:
        tok = c * L + lax.broadcasted_iota(jnp.int32, (L, 1), 0)
        dt = jnp.where(tok < n_valid, dt, 0.0)
        tokt = c * L + lax.broadcasted_iota(jnp.int32, (1, L), 1)
        dtt = jnp.where(tokt < n_valid, dtt, 0.0)
    da = dt * (-jnp.exp(alog_ref[...]))
    dat = dtt * (-jnp.exp(alogt_ref[...]))
    ltri = ltri_ref[...]
    utri = utri_ref[...]
    a = sum(_dot(ltri, part) for part in _split3(da))
    at = sum(_dot(part, utri) for part in _split3(dat))
    a_last = a[L - 1:L, :]
    w = jnp.exp(a_last - a) * dt

    def expand(v, e_ref):
        return _dot(jnp.concatenate(_split3(v), axis=-1), e_ref[...])

    a128 = expand(a, e128_ref)
    a64 = expand(a, e64_ref)
    w64 = expand(w, e64_ref)
    ea64 = jnp.exp(a64)
    xw = (xs * w64).astype(BF16)

    row = lax.broadcasted_iota(jnp.int32, (L, L), 0)
    col = lax.broadcasted_iota(jnp.int32, (L, L), 1)
    causal = col <= row
    lane = lax.broadcasted_iota(jnp.int32, (1, LANES), 1)
    low = lane < SSM_HEAD_DIM

    ys = []
    for g in range(N_GROUPS):
        bg = xact[:, D_INNER + g * D_STATE:D_INNER + (g + 1) * D_STATE]
        cg = xact[:, D_INNER + (N_GROUPS + g) * D_STATE:D_INNER + (N_GROUPS + g + 1) * D_STATE]
        bgb = bg.astype(BF16)
        cgb = cg.astype(BF16)
        gmat = _dot_nt(cgb, bgb)
        gs = slice(g * GROUP_WIDTH, (g + 1) * GROUP_WIDTH)
        st_g = st_ref[:, gs]
        y_off = _dot(cgb, st_g.astype(BF16)) * ea64[:, gs]
        y_diag = []
        for pr in range(HEADS_PER_GROUP // 2):
            ms = []
            for hh in range(2):
                h = g * HEADS_PER_GROUP + 2 * pr + hh
                seg = a128[:, h * LANES:(h + 1) * LANES] - at[h:h + 1, :]
                lm = jnp.where(causal, jnp.exp(seg), 0.0)
                ms.append((gmat * lm * dtt[h:h + 1, :]).astype(BF16))
            h0 = g * HEADS_PER_GROUP + 2 * pr
            xpair = xs[:, h0 * SSM_HEAD_DIM:(h0 + 2) * SSM_HEAD_DIM]
            xstack = jnp.concatenate([jnp.where(low, xpair, 0.0), jnp.where(low, 0.0, xpair)],
                                     axis=0).astype(BF16)
            y_diag.append(_dot(jnp.concatenate(ms, axis=-1), xstack))
        ys.append(jnp.concatenate(y_diag, axis=-1) + y_off)
        cd = ea64[L - 1:L, gs]
        st_ref[:, gs] = cd * st_g + _dot(bg.T.astype(BF16), xw[:, gs])
    y = jnp.concatenate(ys, axis=-1) + dskip_ref[...] * xs
    zz = z_ref[0].astype(F32)
    y = y * (zz * jax.nn.sigmoid(zz))
    outs = []
    for g in range(N_GROUPS):
        yg = y[:, g * GROUP_WIDTH:(g + 1) * GROUP_WIDTH]
        ms = jnp.mean(yg * yg, axis=-1, keepdims=True)
        outs.append(yg * lax.rsqrt(ms + EPS) * ng_ref[:, g * GROUP_WIDTH:(g + 1) * GROUP_WIDTH])
    y_ref[0] = jnp.concatenate(outs, axis=-1).astype(y_ref.dtype)

    @pl.when(c == pl.num_programs(1) - 1)
    def _():
        sout_ref[0] = st_ref[...].T


def _ssd_constants(chunk):
    idx = np.arange(chunk)
    ltri = (idx[None, :] <= idx[:, None]).astype(np.float32)
    utri = ltri.T
    k3 = np.arange(3 * N_SSM_HEADS) % N_SSM_HEADS
    e64 = (k3[:, None] == (np.arange(D_INNER)[None, :] // SSM_HEAD_DIM)).astype(np.float32)
    e128 = (k3[:, None] == (np.arange(N_SSM_HEADS * LANES)[None, :] // LANES)).astype(np.float32)
    return tuple(jnp.asarray(m, dtype=BF16) for m in (ltri, utri, e64, e128))


def _ssd(xbc, z, dt_raw, conv_init, ssm_init, p, n_valid):
    b, t, _ = xbc.shape
    chunk = SSD_CHUNK
    nc = t // chunk
    has_init = ssm_init is not None
    dtt = jnp.swapaxes(dt_raw[:, :, :N_SSM_HEADS], 1, 2)
    cinit = jnp.zeros((b, SUBLANES, CONV_DIM), F32)
    cinit = cinit.at[:, SUBLANES - (CONV_W - 1):, :].set(conv_init)
    consts = _ssd_constants(chunk)
    bmap = lambda bi, c: (bi, c, 0)
    const2 = lambda bi, c: (0, 0)
    args = [xbc, z, dt_raw, dtt, cinit]
    in_specs = [
        pl.BlockSpec((1, chunk, CONV_DIM), bmap),
        pl.BlockSpec((1, chunk, D_INNER), bmap),
        pl.BlockSpec((1, chunk, LANES), bmap),
        pl.BlockSpec((1, N_SSM_HEADS, chunk), lambda bi, c: (bi, 0, c)),
        pl.BlockSpec((1, SUBLANES, CONV_DIM), lambda bi, c: (bi, 0, 0)),
    ]
    if has_init:
        args.append(ssm_init.reshape(b, D_INNER, D_STATE))
        in_specs.append(pl.BlockSpec((1, D_INNER, D_STATE), lambda bi, c: (bi, 0, 0)))
    params = [p['conv_w'], p['conv_b'][None, :], p['dt_bias'][None, :], p['dt_bias'][:, None],
              p['a_log'][None, :], p['a_log'][:, None],
              jnp.repeat(p['d_skip'], SSM_HEAD_DIM)[None, :], p['ssm_norm_g'][None, :]]
    args += params + list(consts)
    in_specs += [pl.BlockSpec(a.shape, const2) for a in params + list(consts)]
    kern = functools.partial(_ssd_kernel, chunk=chunk, n_valid=n_valid, has_init=has_init)
    y, s_out = pl.pallas_call(
        kern,
        out_shape=[jax.ShapeDtypeStruct((b, t, D_INNER), BF16),
                   jax.ShapeDtypeStruct((b, D_INNER, D_STATE), F32)],
        grid=(b, nc),
        in_specs=in_specs,
        out_specs=[pl.BlockSpec((1, chunk, D_INNER), bmap),
                   pl.BlockSpec((1, D_INNER, D_STATE), lambda bi, c: (bi, 0, 0))],
        scratch_shapes=[pltpu.VMEM((SUBLANES, CONV_DIM), F32),
                        pltpu.VMEM((D_STATE, D_INNER), F32)],
        compiler_params=_cparams(("parallel", "arbitrary")),
        name="ssd",
    )(*args)
    return y, s_out.reshape(b, N_SSM_HEADS, SSM_HEAD_DIM, D_STATE)


def _merge_kernel(o_ref, y_ref, ga_ref, gb_ref, x_ref, wa_ref, wb_ref, wo_ref, out_ref):
    ba = _dot(o_ref[...], wa_ref[...])
    bb = _dot(y_ref[...], wb_ref[...])
    mix = (jax.nn.sigmoid(ga_ref[...].astype(F32)) * ba
           + jax.nn.sigmoid(gb_ref[...].astype(F32)) * bb)
    out_ref[...] = x_ref[...] + _dot(mix.astype(BF16), wo_ref[...])


def _merge(o, y, ga, gb, x, wa, wb, wo, tm):
    n = x.shape[0]
    row = lambda i: (i, 0)
    const = lambda i: (0, 0)
    return pl.pallas_call(
        _merge_kernel,
        out_shape=jax.ShapeDtypeStruct((n, D_MODEL), F32),
        grid=(n // tm,),
        in_specs=[
            pl.BlockSpec((tm, ATT_WIDTH), row),
            pl.BlockSpec((tm, D_INNER), row),
            pl.BlockSpec((tm, D_MODEL), row),
            pl.BlockSpec((tm, D_MODEL), row),
            pl.BlockSpec((tm, D_MODEL), row),
            pl.BlockSpec((ATT_WIDTH, D_MODEL), const),
            pl.BlockSpec((D_INNER, D_MODEL), const),
            pl.BlockSpec((D_MODEL, D_MODEL), const),
        ],
        out_specs=pl.BlockSpec((tm, D_MODEL), row),
        compiler_params=_cparams(("parallel",)),
        name="merge",
    )(o, y, ga, gb, x, wa, wb, wo)


def _mlp_kernel(x_ref, g_ref, wu_ref, wd_ref, out_ref, h_ref, acc_ref):
    k = pl.program_id(1)

    @pl.when(k == 0)
    def _():
        x = x_ref[...]
        ms = jnp.mean(x * x, axis=-1, keepdims=True)
        h_ref[...] = (x * lax.rsqrt(ms + EPS) * g_ref[...]).astype(BF16)
        acc_ref[...] = x

    u = _dot(h_ref[...], wu_ref[...])
    r = jnp.square(jnp.maximum(u, 0.0))
    acc_ref[...] += _dot(r.astype(BF16), wd_ref[...])

    @pl.when(k == pl.num_programs(1) - 1)
    def _():
        out_ref[...] = acc_ref[...]


def _mlp(x, g, wu, wd, tm, tk):
    n = x.shape[0]
    return pl.pallas_call(
        _mlp_kernel,
        out_shape=jax.ShapeDtypeStruct((n, D_MODEL), F32),
        grid=(n // tm, D_FF // tk),
        in_specs=[
            pl.BlockSpec((tm, D_MODEL), lambda i, k: (i, 0)),
            pl.BlockSpec((1, D_MODEL), lambda i, k: (0, 0)),
            pl.BlockSpec((D_MODEL, tk), lambda i, k: (0, k)),
            pl.BlockSpec((tk, D_MODEL), lambda i, k: (k, 0)),
        ],
        out_specs=pl.BlockSpec((tm, D_MODEL), lambda i, k: (i, 0)),
        scratch_shapes=[pltpu.VMEM((tm, D_MODEL), BF16), pltpu.VMEM((tm, D_MODEL), F32)],
        compiler_params=_cparams(("parallel", "arbitrary")),
        name="mlp",
    )(x, g, wu, wd)


def _rope_tables(pos):
    half = HEAD_DIM // 2
    inv = ROPE_THETA ** (-jnp.arange(half, dtype=F32) / half)
    ang = pos.astype(F32)[:, None] * inv[None, :]
    cos, sin = jnp.cos(ang), jnp.sin(ang)
    cos128 = jnp.tile(cos, (1, LANES // half))
    sin128 = jnp.tile(jnp.concatenate([-sin, sin], axis=-1), (1, LANES // HEAD_DIM))
    return cos128, sin128


def _pack_weights(p):
    sizes = (ATT_WIDTH, ATT_WIDTH, ATT_WIDTH, D_INNER, CONV_DIM, N_SSM_HEADS, D_MODEL, D_MODEL)
    offs = np.concatenate([[0], np.cumsum(sizes)])
    sec = [p['w_in'][:, offs[i]:offs[i + 1]] for i in range(len(sizes))]
    w_main = jnp.concatenate(sec[:5] + sec[6:], axis=1).astype(BF16)
    w_dt = jnp.pad(sec[5], ((0, 0), (0, LANES - N_SSM_HEADS))).astype(BF16)
    return dict(
        w_main=w_main, w_dt=w_dt,
        wa=p['w_branch_a'].astype(BF16), wb=p['w_branch_b'].astype(BF16),
        wo=p['w_out'].astype(BF16), wu=p['w_up'].astype(BF16), wd=p['w_down'].astype(BF16),
    )


def _mixer_common(x2d, pos_tables, p, w, tm, tn, seq_len, act_dtype):
    idx = np.arange(LANES)
    bsel = jnp.asarray((idx[:, None] // HEAD_DIM == idx[None, :] // HEAD_DIM).astype(np.float32),
                       dtype=BF16)
    q_scale = (HEAD_DIM ** -0.5) * LOG2E
    qg = (jnp.tile(p['q_norm_g'], LANES // HEAD_DIM) * q_scale)[None, :]
    kg = jnp.tile(p['k_norm_g'], LANES // HEAD_DIM)[None, :]
    cos_t, sin_t = pos_tables
    return _in_proj(x2d, p['norm_mix_g'][None, :], w['w_main'], w['w_dt'], qg, kg, cos_t, sin_t,
                    bsel, tm, tn, seq_len, act_dtype)


def _finish(o, y_ssm, ga, gb, x2d, p, w, tm):
    x1 = _merge(o, y_ssm, ga, gb, x2d, w['wa'], w['wb'], w['wo'], tm)
    return _mlp(x1, p['norm_mlp_g'][None, :], w['wu'], w['wd'], tm, 1024)


def _lam(p, lam_init):
    lam = (jnp.exp(jnp.sum(p['lambda_q1'].astype(F32) * p['lambda_k1'].astype(F32)))
           - jnp.exp(jnp.sum(p['lambda_q2'].astype(F32) * p['lambda_k2'].astype(F32))) + lam_init)
    return lam.reshape(1, 1).astype(F32)


def _prompt_layer(x, p, w, lam, lam_init):
    b, t, _ = x.shape
    n = b * t
    tm = min(512, t)
    x2d = x.reshape(n, D_MODEL)
    tables = _rope_tables(jnp.arange(t))
    assert t >= SUBLANES
    q, ktf, vf, z, xbc, xtail, dt, ga, gb = _mixer_common(x2d, tables, p, w, min(1024, t), 512, t,
                                                          BF16)
    blk = min(512, t)
    o = _prompt_attention(q.reshape(b, t, -1), ktf, vf.reshape(b, t, -1), lam,
                          p['subln_g'][None, :], 1.0 - lam_init, blk)
    conv0 = jnp.zeros((b, CONV_W - 1, CONV_DIM), F32)
    y_ssm, new_ssm = _ssd(xbc.reshape(b, t, CONV_DIM), z.reshape(b, t, D_INNER),
                          dt.reshape(b, t, LANES), conv0, None, p, None)
    y = _finish(o.reshape(n, -1), y_ssm.reshape(n, -1), ga, gb, x2d, p, w, tm)
    new_conv = xtail[:, SUBLANES - (CONV_W - 1):]
    k_out = jnp.transpose(ktf.reshape(b, N_HEADS, 2, HEAD_DIM, t), (0, 4, 1, 2, 3))
    return (y.reshape(b, t, D_MODEL), k_out, vf.reshape(b, t, N_HEADS, V_DIM), new_conv, new_ssm)


def _sample_layer(x, p, w, lam, lam_init, cache_k, cache_v, page_table, layer, conv_state,
                  ssm_state):
    b, t, _ = x.shape
    assert t <= DEC_ROWS
    n = b * t
    tm = n
    past = page_table.shape[1] * PAGE_SIZE
    x2d = x.reshape(n, D_MODEL)
    cos_t, sin_t = _rope_tables(past + jnp.arange(t))
    tables = (jnp.tile(cos_t, (b, 1)), jnp.tile(sin_t, (b, 1)))
    q, kf, vf, z, xbc, _, dt, ga, gb = _mixer_common(x2d, tables, p, w, tm, 1024, None, F32)

    q5 = q.reshape(b, t, N_HEADS, 2, HEAD_DIM)
    q5 = jnp.pad(q5, ((0, 0), (0, DEC_ROWS - t), (0, 0), (0, 0), (0, 0)))
    eye_c = jnp.eye(2, dtype=BF16)
    wq = jnp.einsum('bthcd,cC->bhctCd', q5, eye_c).reshape(b, N_HEADS, QROWS, V_DIM).astype(F32)
    pad_t = lambda a: jnp.pad(a.reshape(b, t, -1), ((0, 0), (0, DEC_ROWS - t), (0, 0)))
    ckt = jnp.transpose(cache_k, (0, 1, 3, 4, 5, 2)).reshape(
        cache_k.shape[0], cache_k.shape[1], N_HEADS, V_DIM, PAGE_SIZE)
    cv = cache_v.reshape(cache_v.shape[0], cache_v.shape[1], PAGE_SIZE * N_HEADS, V_DIM)
    o = _sample_attention(page_table, lam, wq, pad_t(kf), pad_t(vf), p['subln_g'][None, :], ckt,
                          cv, layer, t, 1.0 - lam_init)
    o = o[:, :t].reshape(n, ATT_WIDTH).astype(BF16)

    tpad = SSD_CHUNK
    pad_c = lambda a: jnp.pad(a.reshape(b, t, -1), ((0, 0), (0, tpad - t), (0, 0)))
    xbc3 = xbc.reshape(b, t, CONV_DIM)
    y_ssm, new_ssm = _ssd(pad_c(xbc), pad_c(z), pad_c(dt), conv_state, ssm_state, p, t)
    y_ssm = y_ssm[:, :t].reshape(n, D_INNER)
    y = _finish(o, y_ssm, ga, gb, x2d, p, w, tm)
    new_conv = jnp.concatenate([conv_state, xbc3], axis=1)[:, -(CONV_W - 1):]
    return (y.reshape(b, t, D_MODEL), kf.reshape(b, t, N_HEADS, 2, HEAD_DIM),
            vf.reshape(b, t, N_HEADS, V_DIM), new_conv, new_ssm)


def kernel(x_prompt, x_sample, cache_k, cache_v, state_conv, state_ssm, page_table, norm_mix_g, w_in, q_norm_g, k_norm_g, lambda_q1, lambda_k1, lambda_q2, lambda_k2, subln_g, conv_w, conv_b, dt_bias, a_log, d_skip, ssm_norm_g, w_branch_a, w_branch_b, w_out, norm_mlp_g, w_up, w_down):
    depth = w_in.shape[0]
    yp, ys = x_prompt, x_sample
    outs = [[] for _ in range(8)]
    for l in range(depth):
        p = {
            'norm_mix_g': norm_mix_g[l], 'w_in': w_in[l], 'q_norm_g': q_norm_g[l], 'k_norm_g': k_norm_g[l],
            'lambda_q1': lambda_q1[l], 'lambda_k1': lambda_k1[l], 'lambda_q2': lambda_q2[l], 'lambda_k2': lambda_k2[l],
            'subln_g': subln_g[l], 'conv_w': conv_w[l], 'conv_b': conv_b[l], 'dt_bias': dt_bias[l],
            'a_log': a_log[l], 'd_skip': d_skip[l], 'ssm_norm_g': ssm_norm_g[l],
            'w_branch_a': w_branch_a[l], 'w_branch_b': w_branch_b[l], 'w_out': w_out[l],
            'norm_mlp_g': norm_mlp_g[l], 'w_up': w_up[l], 'w_down': w_down[l],
        }
        lam_init = 0.8 - 0.6 * math.exp(-0.3 * l)
        lam = _lam(p, lam_init)
        w = _pack_weights(p)
        yp, kp, vp, cp, sp = _prompt_layer(yp, p, w, lam, lam_init)
        ys, ks, vs, cs, ss = _sample_layer(ys, p, w, lam, lam_init, cache_k, cache_v, page_table, l,
                                           state_conv[l], state_ssm[l])
        for lst, val in zip(outs, (kp, vp, cp, sp, ks, vs, cs, ss)):
            lst.append(val)
    return (yp, ys) + tuple(jnp.stack(o) for o in outs)
```

```python
import functools
import math

import numpy as np
import jax
import jax.numpy as jnp
from jax import lax
from jax.experimental import pallas as pl
from jax.experimental.pallas import tpu as pltpu

F32 = jnp.float32
BF16 = jnp.bfloat16

D_MODEL = 1024
N_HEADS = 8
HEAD_DIM = 64
V_DIM = 2 * HEAD_DIM
ATT_WIDTH = N_HEADS * V_DIM
ROPE_THETA = 10000.0
D_INNER = 2 * D_MODEL
SSM_HEAD_DIM = 64
N_SSM_HEADS = D_INNER // SSM_HEAD_DIM
N_GROUPS = 4
HEADS_PER_GROUP = N_SSM_HEADS // N_GROUPS
GROUP_WIDTH = D_INNER // N_GROUPS
D_STATE = 128
CONV_W = 4
CONV_DIM = D_INNER + 2 * N_GROUPS * D_STATE
SSD_CHUNK = 128
D_FF = 4 * D_MODEL
PAGE_SIZE = 128
EPS = 1e-6
LOG2E = 1.4426950408889634

LANES = 128
SUBLANES = 8
VMEM_LIMIT_BYTES = 56 * 1024 * 1024

NEG_BIG = -1e30

_SECTION_WIDTHS = (ATT_WIDTH, ATT_WIDTH, ATT_WIDTH, D_INNER, CONV_DIM, D_MODEL, D_MODEL)


def _section_starts(tn):
    return tuple(int(s) // tn for s in np.cumsum((0,) + _SECTION_WIDTHS))

PAGES_PER_STEP = 8
DEC_ROWS = 8
QROWS = 2 * DEC_ROWS


def _cparams(sem):
    return pltpu.CompilerParams(dimension_semantics=sem, vmem_limit_bytes=VMEM_LIMIT_BYTES)


def _dot(a, b):
    return jnp.dot(a, b, preferred_element_type=F32)


def _dot_nt(a, b):
    return lax.dot_general(a, b, (((1,), (1,)), ((), ())), preferred_element_type=F32)


def _exp2(x):
    return jnp.exp2(x)


def _rope_norm(acc, gain, cos, sin, bsel):
    lane = lax.broadcasted_iota(jnp.int32, (1, LANES), 1)
    first_half = (lane % HEAD_DIM) < (HEAD_DIM // 2)
    outs = []
    for hb in range(acc.shape[1] // LANES):
        xh = acc[:, hb * LANES:(hb + 1) * LANES]
        ss = _dot((xh * xh).astype(BF16), bsel)
        y = xh * lax.rsqrt(ss * (1.0 / HEAD_DIM) + EPS) * gain
        partner = jnp.where(first_half,
                            pltpu.roll(y, LANES - HEAD_DIM // 2, 1),
                            pltpu.roll(y, HEAD_DIM // 2, 1))
        outs.append(y * cos + partner * sin)
    return jnp.concatenate(outs, axis=-1)


def _in_proj_kernel(x_ref, g_ref, wf_ref, wg_ref, wdt_ref, qg_ref, kg_ref, cos_ref, sin_ref, bsel_ref,
                    q_ref, k_ref, v_ref, z_ref, xbc_ref, xtail_ref, dt_ref, ga_ref, gb_ref,
                    h_ref, *, starts, transpose_k):
    J_Q, J_K, J_V, J_Z, J_X, J_GA, J_GB, _ = starts
    j = pl.program_id(1)

    @pl.when(j == 0)
    def _():
        x = x_ref[...]
        ms = jnp.mean(x * x, axis=-1, keepdims=True)
        h = x * lax.rsqrt(ms + EPS) * g_ref[...]
        h_ref[...] = h.astype(BF16)
        dt_ref[...] = _dot(h_ref[...], wdt_ref[...])

    @pl.when(j < J_GA)
    def _():
        acc = _dot(h_ref[...], wf_ref[...])

        @pl.when(j < J_K)
        def _():
            r = _rope_norm(acc, qg_ref[...], cos_ref[...], sin_ref[...], bsel_ref[...])
            q_ref[...] = r.astype(BF16)

        @pl.when((j >= J_K) & (j < J_V))
        def _():
            r = _rope_norm(acc, kg_ref[...], cos_ref[...], sin_ref[...], bsel_ref[...])
            if transpose_k:
                k_ref[0] = r.T
            else:
                k_ref[...] = r

        @pl.when((j >= J_V) & (j < J_Z))
        def _():
            v_ref[...] = acc

        @pl.when((j >= J_Z) & (j < J_X))
        def _():
            z_ref[...] = acc.astype(z_ref.dtype)

        @pl.when(j >= J_X)
        def _():
            xbc_ref[...] = acc.astype(xbc_ref.dtype)
            xtail_ref[0] = acc[acc.shape[0] - SUBLANES:, :]

    @pl.when(j >= J_GA)
    def _():
        acc = _dot(h_ref[...], wg_ref[...])

        @pl.when(j < J_GB)
        def _():
            ga_ref[...] = acc.astype(ga_ref.dtype)

        @pl.when(j >= J_GB)
        def _():
            gb_ref[...] = acc.astype(gb_ref.dtype)


def _in_proj(x, g, w_front, w_gate, w_dt, qg, kg, cos_t, sin_t, bsel, tm, tn, seq_len, act_dtype):
    n = x.shape[0]
    rope_blocks = cos_t.shape[0] // tm
    starts = _section_starts(tn)
    J_Q, J_K, J_V, J_Z, J_X, J_GA, J_GB, n_tiles = starts
    grid = (n // tm, n_tiles)
    row = lambda i, j: (i, 0)
    const = lambda i, j: (0, 0)
    rope_map = lambda i, j: (i % rope_blocks, 0)
    in_specs = [
        pl.BlockSpec((tm, D_MODEL), row),
        pl.BlockSpec((1, D_MODEL), const),
        pl.BlockSpec((D_MODEL, tn), lambda i, j: (0, jnp.minimum(j, J_GA - 1))),
        pl.BlockSpec((D_MODEL, tn), lambda i, j: (0, jnp.maximum(j - J_GA, 0))),
        pl.BlockSpec((D_MODEL, LANES), const),
        pl.BlockSpec((1, LANES), const),
        pl.BlockSpec((1, LANES), const),
        pl.BlockSpec((tm, LANES), rope_map),
        pl.BlockSpec((tm, LANES), rope_map),
        pl.BlockSpec((LANES, LANES), const),
    ]
    def section(start, stop):
        return lambda i, j: (i, jnp.clip(j - start, 0, stop - start - 1))

    transpose_k = seq_len is not None
    tiles_per_seq = seq_len // tm if transpose_k else 1
    if transpose_k:
        k_shape = (n // seq_len, ATT_WIDTH, seq_len)
        k_spec = pl.BlockSpec(
            (1, tn, tm),
            lambda i, j: (i // tiles_per_seq, jnp.clip(j - J_K, 0, J_V - J_K - 1), i % tiles_per_seq))
    else:
        k_shape = (n, ATT_WIDTH)
        k_spec = pl.BlockSpec((tm, tn), section(J_K, J_V))
    out_shape = [
        jax.ShapeDtypeStruct((n, ATT_WIDTH), BF16),
        jax.ShapeDtypeStruct(k_shape, F32),
        jax.ShapeDtypeStruct((n, ATT_WIDTH), F32),
        jax.ShapeDtypeStruct((n, D_INNER), act_dtype),
        jax.ShapeDtypeStruct((n, CONV_DIM), act_dtype),
        jax.ShapeDtypeStruct((n // tm, SUBLANES, CONV_DIM), F32),
        jax.ShapeDtypeStruct((n, LANES), F32),
        jax.ShapeDtypeStruct((n, D_MODEL), act_dtype),
        jax.ShapeDtypeStruct((n, D_MODEL), act_dtype),
    ]
    out_specs = [
        pl.BlockSpec((tm, tn), section(J_Q, J_K)),
        k_spec,
        pl.BlockSpec((tm, tn), section(J_V, J_Z)),
        pl.BlockSpec((tm, tn), section(J_Z, J_X)),
        pl.BlockSpec((tm, tn), section(J_X, J_GA)),
        pl.BlockSpec((1, SUBLANES, tn),
                     lambda i, j: (i, 0, jnp.clip(j - J_X, 0, J_GA - J_X - 1))),
        pl.BlockSpec((tm, LANES), row),
        pl.BlockSpec((tm, tn), section(J_GA, J_GB)),
        pl.BlockSpec((tm, tn), section(J_GB, n_tiles)),
    ]
    return pl.pallas_call(
        functools.partial(_in_proj_kernel, starts=starts, transpose_k=transpose_k),
        out_shape=out_shape,
        grid=grid,
        in_specs=in_specs,
        out_specs=out_specs,
        scratch_shapes=[pltpu.VMEM((tm, D_MODEL), BF16)],
        compiler_params=_cparams(("parallel", "arbitrary")),
        name="in_proj",
    )(x, g, w_front, w_gate, w_dt, qg, kg, cos_t, sin_t, bsel)


def _subln(o, g, out_scale):
    ms = jnp.mean(o * o, axis=-1, keepdims=True)
    return o * lax.rsqrt(ms + EPS) * g * out_scale


def _prompt_attn_kernel(lam_ref, q_ref, kt_ref, v_ref, g_ref, o_ref, *, blk, out_scale):
    t = q_ref.shape[1]
    lam = lam_ref[0, 0]
    row = lax.broadcasted_iota(jnp.int32, (blk, blk), 0)
    col = lax.broadcasted_iota(jnp.int32, (blk, blk), 1)
    causal = col <= row
    kt = kt_ref[0].astype(BF16)
    v = v_ref[0].astype(BF16)
    for qi in range(t // blk):
        r0 = qi * blk
        q = q_ref[0, r0:r0 + blk, :]
        ps = []
        for c in range(2):
            cs = slice(c * HEAD_DIM, (c + 1) * HEAD_DIM)
            qc = q[:, cs]
            s_d = jnp.where(causal, _dot(qc, kt[cs, r0:r0 + blk]), NEG_BIG)
            m = jnp.max(s_d, axis=-1, keepdims=True)
            p_o = None
            if qi > 0:
                s_o = _dot(qc, kt[cs, 0:r0])
                m = jnp.maximum(m, jnp.max(s_o, axis=-1, keepdims=True))
                p_o = _exp2(s_o - m)
            p_d = _exp2(s_d - m)
            l = jnp.sum(p_d, axis=-1, keepdims=True)
            if qi > 0:
                l = l + jnp.sum(p_o, axis=-1, keepdims=True)
            ps.append((p_o, p_d, l))
        w0 = 1.0 / ps[0][2]
        w1 = lam / ps[1][2]
        o = _dot((ps[0][1] * w0 - ps[1][1] * w1).astype(BF16), v[r0:r0 + blk, :])
        if qi > 0:
            o = o + _dot((ps[0][0] * w0 - ps[1][0] * w1).astype(BF16), v[0:r0, :])
        o_ref[0, r0:r0 + blk, :] = _subln(o, g_ref[...], out_scale).astype(o_ref.dtype)


def _prompt_attention(q, kt, v, lam, subln_g, out_scale, blk):
    b, t, _ = q.shape
    kern = functools.partial(_prompt_attn_kernel, blk=blk, out_scale=out_scale)
    return pl.pallas_call(
        kern,
        out_shape=jax.ShapeDtypeStruct((b, t, ATT_WIDTH), BF16),
        grid=(b, N_HEADS),
        in_specs=[
            pl.BlockSpec(memory_space=pltpu.SMEM),
            pl.BlockSpec((1, t, V_DIM), lambda bi, h: (bi, 0, h)),
            pl.BlockSpec((1, V_DIM, t), lambda bi, h: (bi, h, 0)),
            pl.BlockSpec((1, t, V_DIM), lambda bi, h: (bi, 0, h)),
            pl.BlockSpec((1, V_DIM), lambda bi, h: (0, 0)),
        ],
        out_specs=pl.BlockSpec((1, t, V_DIM), lambda bi, h: (bi, 0, h)),
        compiler_params=_cparams(("parallel", "parallel")),
        name="prompt_attention",
    )(lam, q, kt, v, subln_g)


def _sample_attn_kernel(pt_ref, lam_ref, wq_ref, knew_ref, vnew_ref, g_ref, *rest,
                        n_valid, out_scale):
    kp_refs = rest[:PAGES_PER_STEP]
    vp_refs = rest[PAGES_PER_STEP:2 * PAGES_PER_STEP]
    o_ref = rest[2 * PAGES_PER_STEP]
    m_ref, l_ref, acc_ref = rest[2 * PAGES_PER_STEP + 1:]
    j = pl.program_id(1)

    @pl.when(j == 0)
    def _():
        m_ref[...] = jnp.full(m_ref.shape, NEG_BIG, F32)
        l_ref[...] = jnp.zeros(l_ref.shape, F32)
        acc_ref[...] = jnp.zeros(acc_ref.shape, F32)

    s = jnp.concatenate(
        [jnp.concatenate([_dot(wq_ref[0, h], kp_refs[p][0, 0, h]) for p in range(PAGES_PER_STEP)],
                         axis=-1) for h in range(N_HEADS)], axis=0)
    m = m_ref[...]
    m_new = jnp.maximum(m, jnp.max(s, axis=-1, keepdims=True))
    alpha = _exp2(m - m_new)
    pr = _exp2(s - m_new)
    l_ref[...] = alpha * l_ref[...] + jnp.sum(pr, axis=-1, keepdims=True)
    m_ref[...] = m_new
    pvs = []
    for h in range(N_HEADS):
        ph = pr[h * QROWS:(h + 1) * QROWS]
        v_rows = pl.ds(h, PAGE_SIZE, stride=N_HEADS)
        pv = _dot(ph[:, :PAGE_SIZE], vp_refs[0][0, 0, v_rows, :])
        for p in range(1, PAGES_PER_STEP):
            pv = pv + _dot(ph[:, p * PAGE_SIZE:(p + 1) * PAGE_SIZE], vp_refs[p][0, 0, v_rows, :])
        pvs.append(pv)
    acc_ref[...] = alpha * acc_ref[...] + jnp.concatenate(pvs, axis=0)

    @pl.when(j == pl.num_programs(1) - 1)
    def _():
        lam = lam_ref[0, 0]
        row_t = lax.broadcasted_iota(jnp.int32, (QROWS, DEC_ROWS), 0) % DEC_ROWS
        col_t = lax.broadcasted_iota(jnp.int32, (QROWS, DEC_ROWS), 1)
        valid = (col_t <= row_t) & (col_t < n_valid)
        outs = []
        for h in range(N_HEADS):
            rs = slice(h * QROWS, (h + 1) * QROWS)
            hs = slice(h * V_DIM, (h + 1) * V_DIM)
            sn = jnp.where(valid, _dot_nt(wq_ref[0, h], knew_ref[0][:, hs]), NEG_BIG)
            mm = m_ref[rs, :]
            mf = jnp.maximum(mm, jnp.max(sn, axis=-1, keepdims=True))
            al = _exp2(mm - mf)
            pn = _exp2(sn - mf)
            lf = al * l_ref[rs, :] + jnp.sum(pn, axis=-1, keepdims=True)
            af = al * acc_ref[rs, :] + _dot(pn, vnew_ref[0][:, hs])
            of = af / lf
            od = of[:DEC_ROWS] - lam * of[DEC_ROWS:]
            outs.append(_subln(od, g_ref[...], out_scale))
        o_ref[0] = jnp.concatenate(outs, axis=-1)


def _sample_attention(page_table, lam, wq, k_new, v_new, subln_g, cache_kt, cache_v, layer,
                      n_valid, out_scale):
    db, n_pages = page_table.shape
    n_steps = n_pages // PAGES_PER_STEP

    def page_map(p, ndim):
        return lambda b, j, pt: (layer, pt[b * n_pages + j * PAGES_PER_STEP + p]) + (0,) * (ndim - 2)

    kern = functools.partial(_sample_attn_kernel, n_valid=n_valid, out_scale=out_scale)
    grid_spec = pltpu.PrefetchScalarGridSpec(
        num_scalar_prefetch=1,
        grid=(db, n_steps),
        in_specs=[
            pl.BlockSpec(memory_space=pltpu.SMEM),
            pl.BlockSpec((1, N_HEADS, QROWS, V_DIM), lambda b, j, pt: (b, 0, 0, 0)),
            pl.BlockSpec((1, DEC_ROWS, ATT_WIDTH), lambda b, j, pt: (b, 0, 0)),
            pl.BlockSpec((1, DEC_ROWS, ATT_WIDTH), lambda b, j, pt: (b, 0, 0)),
            pl.BlockSpec((1, V_DIM), lambda b, j, pt: (0, 0)),
        ] + [pl.BlockSpec((1, 1, N_HEADS, V_DIM, PAGE_SIZE), page_map(p, 5))
             for p in range(PAGES_PER_STEP)]
          + [pl.BlockSpec((1, 1, PAGE_SIZE * N_HEADS, V_DIM), page_map(p, 4))
             for p in range(PAGES_PER_STEP)],
        out_specs=pl.BlockSpec((1, DEC_ROWS, ATT_WIDTH), lambda b, j, pt: (b, 0, 0)),
        scratch_shapes=[
            pltpu.VMEM((N_HEADS * QROWS, 1), F32),
            pltpu.VMEM((N_HEADS * QROWS, 1), F32),
            pltpu.VMEM((N_HEADS * QROWS, V_DIM), F32),
        ],
    )
    return pl.pallas_call(
        kern,
        out_shape=jax.ShapeDtypeStruct((db, DEC_ROWS, ATT_WIDTH), F32),
        grid_spec=grid_spec,
        compiler_params=_cparams(("parallel", "arbitrary")),
        name="sample_attention",
    )(page_table.reshape(-1), lam, wq, k_new, v_new, subln_g,
      *([cache_kt] * PAGES_PER_STEP), *([cache_v] * PAGES_PER_STEP))


def _split3(v):
    hi = v.astype(BF16)
    r1 = v - hi.astype(F32)
    mid = r1.astype(BF16)
    lo = (r1 - mid.astype(F32)).astype(BF16)
    return hi, mid, lo


def _softplus(x):
    return jnp.maximum(x, 0.0) + jnp.log1p(jnp.exp(-jnp.abs(x)))


def _ssd_kernel(*refs, chunk, n_valid, has_init):
    (xbc_ref, z_ref, dt_ref, dtt_ref, cinit_ref) = refs[:5]
    k = 5
    if has_init:
        sinit_ref = refs[k]
        k += 1
    (cw_ref, cb_ref, bias_ref, biast_ref, alog_ref, alogt_ref, dskip_ref, ng_ref,
     ltri_ref, utri_ref, e64_ref, e128_ref) = refs[k:k + 12]
    y_ref, sout_ref = refs[k + 12:k + 14]
    tail_ref, st_ref = refs[k + 14:]
    c = pl.program_id(1)
    L = chunk

    @pl.when(c == 0)
    def _():
        tail_ref[...] = cinit_ref[0]
        if has_init:
            st_ref[...] = sinit_ref[0].T
        else:
            st_ref[...] = jnp.zeros(st_ref.shape, F32)

    xraw = xbc_ref[0].astype(F32)
    head =jnp.concatenate([tail_ref[...], xraw[:SUBLANES]], axis=0)
    conv = cb_ref[...]
    conv_head = cb_ref[...]
    for kk in range(CONV_W):
        back = CONV_W - 1 - kk
        wk = cw_ref[kk:kk + 1, :]
        shifted = pltpu.roll(xraw, back, 0) if back else xraw
        conv = conv + shifted * wk
        conv_head = conv_head + head[SUBLANES - back:2 * SUBLANES - back] * wk
    conv = jnp.concatenate([conv_head, conv[SUBLANES:]], axis=0)
    tail_ref[...] = xraw[L - SUBLANES:, :]
    xact = conv * jax.nn.sigmoid(conv)
    xs = xact[:, :D_INNER]

    nh = N_SSM_HEADS
    dt = _softplus(dt_ref[0][:, :nh] + bias_ref[...])
    dtt = _softplus(dtt_ref[0] + biast_ref[...])
    if n_valid is not None:
        tok = c * L + lax.broadcasted_iota(jnp.int32, (L, 1), 0)
        dt = jnp.where(tok < n_valid, dt, 0.0)
        tokt = c * L + lax.broadcasted_iota(jnp.int32, (1, L), 1)
        dtt = jnp.where(tokt < n_valid, dtt, 0.0)
    da = dt * (-jnp.exp(alog_ref[...]))
    dat = dtt * (-jnp.exp(alogt_ref[...]))
    ltri = ltri_ref[...]
    utri = utri_ref[...]
    a = sum(_dot(ltri, part) for part in _split3(da))
    at = sum(_dot(part, utri) for part in _split3(dat))
    a_last = a[L - 1:L, :]
    w = jnp.exp(a_last - a) * dt

    def expand(v, e_ref):
        return _dot(jnp.concatenate(_split3(v), axis=-1), e_ref[...])

    a128 = expand(a, e128_ref)
    a64 = expand(a, e64_ref)
    w64 = expand(w, e64_ref)
    ea64 = jnp.exp(a64)
    xw = (xs * w64).astype(BF16)

    row = lax.broadcasted_iota(jnp.int32, (L, L), 0)
    col = lax.broadcasted_iota(jnp.int32, (L, L), 1)
    causal = col <= row
    lane = lax.broadcasted_iota(jnp.int32, (1, LANES), 1)
    low = lane < SSM_HEAD_DIM

    ys = []
    for g in range(N_GROUPS):
        bg = xact[:, D_INNER + g * D_STATE:D_INNER + (g + 1) * D_STATE]
        cg = xact[:, D_INNER + (N_GROUPS + g) * D_STATE:D_INNER + (N_GROUPS + g + 1) * D_STATE]
        bgb = bg.astype(BF16)
        cgb = cg.astype(BF16)
        gmat = _dot_nt(cgb, bgb)
        gs = slice(g * GROUP_WIDTH, (g + 1) * GROUP_WIDTH)
        st_g = st_ref[:, gs]
        y_off = _dot(cgb, st_g.astype(BF16)) * ea64[:, gs]
        y_diag = []
        for pr in range(HEADS_PER_GROUP // 2):
            ms = []
            for hh in range(2):
                h = g * HEADS_PER_GROUP + 2 * pr + hh
                seg = a128[:, h * LANES:(h + 1) * LANES] - at[h:h + 1, :]
                lm = jnp.where(causal, jnp.exp(seg), 0.0)
                ms.append((gmat * lm * dtt[h:h + 1, :]).astype(BF16))
            h0 = g * HEADS_PER_GROUP + 2 * pr
            xpair = xs[:, h0 * SSM_HEAD_DIM:(h0 + 2) * SSM_HEAD_DIM]
            xstack = jnp.concatenate([jnp.where(low, xpair, 0.0), jnp.where(low, 0.0, xpair)],
                                     axis=0).astype(BF16)
            y_diag.append(_dot(jnp.concatenate(ms, axis=-1), xstack))
        ys.append(jnp.concatenate(y_diag, axis=-1) + y_off)
        cd = ea64[L - 1:L, gs]
        st_ref[:, gs] = cd * st_g + _dot(bg.T.astype(BF16), xw[:, gs])
    y = jnp.concatenate(ys, axis=-1) + dskip_ref[...] * xs
    zz = z_ref[0].astype(F32)
    y = y * (zz * jax.nn.sigmoid(zz))
    outs = []
    for g in range(N_GROUPS):
        yg = y[:, g * GROUP_WIDTH:(g + 1) * GROUP_WIDTH]
        ms = jnp.mean(yg * yg, axis=-1, keepdims=True)
        outs.append(yg * lax.rsqrt(ms + EPS) * ng_ref[:, g * GROUP_WIDTH:(g + 1) * GROUP_WIDTH])
    y_ref[0] = jnp.concatenate(outs, axis=-1).astype(y_ref.dtype)

    @pl.when(c == pl.num_programs(1) - 1)
    def _():
        sout_ref[0] = st_ref[...].T


def _ssd_constants(chunk):
    idx = np.arange(chunk)
    ltri = (idx[None, :] <= idx[:, None]).astype(np.float32)
    utri = ltri.T
    k3 = np.arange(3 * N_SSM_HEADS) % N_SSM_HEADS
    e64 = (k3[:, None] == (np.arange(D_INNER)[None, :] // SSM_HEAD_DIM)).astype(np.float32)
    e128 = (k3[:, None] == (np.arange(N_SSM_HEADS * LANES)[None, :] // LANES)).astype(np.float32)
    return tuple(jnp.asarray(m, dtype=BF16) for m in (ltri, utri, e64, e128))


def _ssd(xbc, z, dt_raw, conv_init, ssm_init, p, n_valid):
    b, t, _ = xbc.shape
    chunk = SSD_CHUNK
    nc = t // chunk
    has_init = ssm_init is not None
    dtt = jnp.swapaxes(dt_raw[:, :, :N_SSM_HEADS], 1, 2)
    cinit = jnp.zeros((b, SUBLANES, CONV_DIM), F32)
    cinit = cinit.at[:, SUBLANES - (CONV_W - 1):, :].set(conv_init)
    consts = _ssd_constants(chunk)
    bmap = lambda bi, c: (bi, c, 0)
    const2 = lambda bi, c: (0, 0)
    args = [xbc, z, dt_raw, dtt, cinit]
    in_specs = [
        pl.BlockSpec((1, chunk, CONV_DIM), bmap),
        pl.BlockSpec((1, chunk, D_INNER), bmap),
        pl.BlockSpec((1, chunk, LANES), bmap),
        pl.BlockSpec((1, N_SSM_HEADS, chunk), lambda bi, c: (bi, 0, c)),
        pl.BlockSpec((1, SUBLANES, CONV_DIM), lambda bi, c: (bi, 0, 0)),
    ]
    if has_init:
        args.append(ssm_init.reshape(b, D_INNER, D_STATE))
        in_specs.append(pl.BlockSpec((1, D_INNER, D_STATE), lambda bi, c: (bi, 0, 0)))
    params = [p['conv_w'], p['conv_b'][None, :], p['dt_bias'][None, :], p['dt_bias'][:, None],
              p['a_log'][None, :], p['a_log'][:, None],
              jnp.repeat(p['d_skip'], SSM_HEAD_DIM)[None, :], p['ssm_norm_g'][None, :]]
    args += params + list(consts)
    in_specs += [pl.BlockSpec(a.shape, const2) for a in params + list(consts)]
    kern = functools.partial(_ssd_kernel, chunk=chunk, n_valid=n_valid, has_init=has_init)
    y, s_out = pl.pallas_call(
        kern,
        out_shape=[jax.ShapeDtypeStruct((b, t, D_INNER), BF16),
                   jax.ShapeDtypeStruct((b, D_INNER, D_STATE), F32)],
        grid=(b, nc),
        in_specs=in_specs,
        out_specs=[pl.BlockSpec((1, chunk, D_INNER), bmap),
                   pl.BlockSpec((1, D_INNER, D_STATE), lambda bi, c: (bi, 0, 0))],
        scratch_shapes=[pltpu.VMEM((SUBLANES, CONV_DIM), F32),
                        pltpu.VMEM((D_STATE, D_INNER), F32)],
        compiler_params=_cparams(("parallel", "arbitrary")),
        name="ssd",
    )(*args)
    return y, s_out.reshape(b, N_SSM_HEADS, SSM_HEAD_DIM, D_STATE)


def _merge_kernel(o_ref, y_ref, ga_ref, gb_ref, x_ref, wa_ref, wb_ref, wo_ref, out_ref):
    ba = _dot(o_ref[...], wa_ref[...])
    bb = _dot(y_ref[...], wb_ref[...])
    mix = (jax.nn.sigmoid(ga_ref[...].astype(F32)) * ba
           + jax.nn.sigmoid(gb_ref[...].astype(F32)) * bb)
    out_ref[...] = x_ref[...] + _dot(mix.astype(BF16), wo_ref[...])


def _merge(o, y, ga, gb, x, wa, wb, wo, tm):
    n = x.shape[0]
    row = lambda i: (i, 0)
    const = lambda i: (0, 0)
    return pl.pallas_call(
        _merge_kernel,
        out_shape=jax.ShapeDtypeStruct((n, D_MODEL), F32),
        grid=(n // tm,),
        in_specs=[
            pl.BlockSpec((tm, ATT_WIDTH), row),
            pl.BlockSpec((tm, D_INNER), row),
            pl.BlockSpec((tm, D_MODEL), row),
            pl.BlockSpec((tm, D_MODEL), row),
            pl.BlockSpec((tm, D_MODEL), row),
            pl.BlockSpec((ATT_WIDTH, D_MODEL), const),
            pl.BlockSpec((D_INNER, D_MODEL), const),
            pl.BlockSpec((D_MODEL, D_MODEL), const),
        ],
        out_specs=pl.BlockSpec((tm, D_MODEL), row),
        compiler_params=_cparams(("parallel",)),
        name="merge",
    )(o, y, ga, gb, x, wa, wb, wo)


def _mlp_kernel(x_ref, g_ref, wu_ref, wd_ref, out_ref, h_ref, acc_ref):
    k = pl.program_id(1)

    @pl.when(k == 0)
    def _():
        x = x_ref[...]
        ms = jnp.mean(x * x, axis=-1, keepdims=True)
        h_ref[...] = (x * lax.rsqrt(ms + EPS) * g_ref[...]).astype(BF16)
        acc_ref[...] = x

    u = _dot(h_ref[...], wu_ref[...])
    r = jnp.square(jnp.maximum(u, 0.0))
    acc_ref[...] += _dot(r.astype(BF16), wd_ref[...])

    @pl.when(k == pl.num_programs(1) - 1)
    def _():
        out_ref[...] = acc_ref[...]


def _mlp(x, g, wu, wd, tm, tk):
    n = x.shape[0]
    return pl.pallas_call(
        _mlp_kernel,
        out_shape=jax.ShapeDtypeStruct((n, D_MODEL), F32),
        grid=(n // tm, D_FF // tk),
        in_specs=[
            pl.BlockSpec((tm, D_MODEL), lambda i, k: (i, 0)),
            pl.BlockSpec((1, D_MODEL), lambda i, k: (0, 0)),
            pl.BlockSpec((D_MODEL, tk), lambda i, k: (0, k)),
            pl.BlockSpec((tk, D_MODEL), lambda i, k: (k, 0)),
        ],
        out_specs=pl.BlockSpec((tm, D_MODEL), lambda i, k: (i, 0)),
        scratch_shapes=[pltpu.VMEM((tm, D_MODEL), BF16), pltpu.VMEM((tm, D_MODEL), F32)],
        compiler_params=_cparams(("parallel", "arbitrary")),
        name="mlp",
    )(x, g, wu, wd)


def _rope_tables(pos):
    half = HEAD_DIM // 2
    inv = ROPE_THETA ** (-jnp.arange(half, dtype=F32) / half)
    ang = pos.astype(F32)[:, None] * inv[None, :]
    cos, sin = jnp.cos(ang), jnp.sin(ang)
    cos128 = jnp.tile(cos, (1, LANES // half))
    sin128 = jnp.tile(jnp.concatenate([-sin, sin], axis=-1), (1, LANES // HEAD_DIM))
    return cos128, sin128


def _pack_weights(p):
    dt0 = sum(_SECTION_WIDTHS[:5])
    w_front = p['w_in'].astype(BF16)
    w_gate = w_front[:, dt0 + N_SSM_HEADS:]
    w_dt = jnp.pad(p['w_in'][:, dt0:dt0 + N_SSM_HEADS],
                   ((0, 0), (0, LANES - N_SSM_HEADS))).astype(BF16)
    return dict(
        w_front=w_front, w_gate=w_gate, w_dt=w_dt,
        wa=p['w_branch_a'].astype(BF16), wb=p['w_branch_b'].astype(BF16),
        wo=p['w_out'].astype(BF16), wu=p['w_up'].astype(BF16), wd=p['w_down'].astype(BF16),
    )


def _mixer_common(x2d, pos_tables, p, w, tm, tn, seq_len, act_dtype):
    idx = np.arange(LANES)
    bsel = jnp.asarray((idx[:, None] // HEAD_DIM == idx[None, :] // HEAD_DIM).astype(np.float32),
                       dtype=BF16)
    q_scale = (HEAD_DIM ** -0.5) * LOG2E
    qg = (jnp.tile(p['q_norm_g'], LANES // HEAD_DIM) * q_scale)[None, :]
    kg = jnp.tile(p['k_norm_g'], LANES // HEAD_DIM)[None, :]
    cos_t, sin_t = pos_tables
    return _in_proj(x2d, p['norm_mix_g'][None, :], w['w_front'], w['w_gate'], w['w_dt'], qg, kg,
                    cos_t, sin_t,
                    bsel, tm, tn, seq_len, act_dtype)


def _finish(o, y_ssm, ga, gb, x2d, p, w, tm):
    x1 = _merge(o, y_ssm, ga, gb, x2d, w['wa'], w['wb'], w['wo'], tm)
    return _mlp(x1, p['norm_mlp_g'][None, :], w['wu'], w['wd'], tm, 1024)


def _lam(p, lam_init):
    lam = (jnp.exp(jnp.sum(p['lambda_q1'].astype(F32) * p['lambda_k1'].astype(F32)))
           - jnp.exp(jnp.sum(p['lambda_q2'].astype(F32) * p['lambda_k2'].astype(F32))) + lam_init)
    return lam.reshape(1, 1).astype(F32)


def _prompt_layer(x, p, w, lam, lam_init):
    b, t, _ = x.shape
    n = b * t
    tm = min(512, t)
    x2d = x.reshape(n, D_MODEL)
    tables = _rope_tables(jnp.arange(t))
    assert t >= SUBLANES
    q, ktf, vf, z, xbc, xtail, dt, ga, gb = _mixer_common(x2d, tables, p, w, min(1024, t), 512, t,
                                                          BF16)
    blk = min(512, t)
    o = _prompt_attention(q.reshape(b, t, -1), ktf, vf.reshape(b, t, -1), lam,
                          p['subln_g'][None, :], 1.0 - lam_init, blk)
    conv0 = jnp.zeros((b, CONV_W - 1, CONV_DIM), F32)
    y_ssm, new_ssm = _ssd(xbc.reshape(b, t, CONV_DIM), z.reshape(b, t, D_INNER),
                          dt.reshape(b, t, LANES), conv0, None, p, None)
    y = _finish(o.reshape(n, -1), y_ssm.reshape(n, -1), ga, gb, x2d, p, w, tm)
    new_conv = xtail.reshape(b, -1, SUBLANES, CONV_DIM)[:, -1, SUBLANES - (CONV_W - 1):]
    k_out = jnp.transpose(ktf.reshape(b, N_HEADS, 2, HEAD_DIM, t), (0, 4, 1, 2, 3))
    return (y.reshape(b, t, D_MODEL), k_out, vf.reshape(b, t, N_HEADS, V_DIM), new_conv, new_ssm)


def _sample_layer(x, p, w, lam, lam_init, cache_k, cache_v, page_table, layer, conv_state,
                  ssm_state):
    b, t, _ = x.shape
    assert t <= DEC_ROWS
    n = b * t
    tm = n
    past = page_table.shape[1] * PAGE_SIZE
    x2d = x.reshape(n, D_MODEL)
    cos_t, sin_t = _rope_tables(past + jnp.arange(t))
    tables = (jnp.tile(cos_t, (b, 1)), jnp.tile(sin_t, (b, 1)))
    q, kf, vf, z, xbc, _, dt, ga, gb = _mixer_common(x2d, tables, p, w, tm, 1024, None, F32)

    q5 = q.reshape(b, t, N_HEADS, 2, HEAD_DIM)
    q5 = jnp.pad(q5, ((0, 0), (0, DEC_ROWS - t), (0, 0), (0, 0), (0, 0)))
    eye_c = jnp.eye(2, dtype=BF16)
    wq = jnp.einsum('bthcd,cC->bhctCd', q5, eye_c).reshape(b, N_HEADS, QROWS, V_DIM).astype(F32)
    pad_t = lambda a: jnp.pad(a.reshape(b, t, -1), ((0, 0), (0, DEC_ROWS - t), (0, 0)))
    ckt = jnp.transpose(cache_k, (0, 1, 3, 4, 5, 2)).reshape(
        cache_k.shape[0], cache_k.shape[1], N_HEADS, V_DIM, PAGE_SIZE)
    cv = cache_v.reshape(cache_v.shape[0], cache_v.shape[1], PAGE_SIZE * N_HEADS, V_DIM)
    o = _sample_attention(page_table, lam, wq, pad_t(kf), pad_t(vf), p['subln_g'][None, :], ckt,
                          cv, layer, t, 1.0 - lam_init)
    o = o[:, :t].reshape(n, ATT_WIDTH).astype(BF16)

    tpad = SSD_CHUNK
    pad_c = lambda a: jnp.pad(a.reshape(b, t, -1), ((0, 0), (0, tpad - t), (0, 0)))
    xbc3 = xbc.reshape(b, t, CONV_DIM)
    y_ssm, new_ssm = _ssd(pad_c(xbc), pad_c(z), pad_c(dt), conv_state, ssm_state, p, t)
    y_ssm = y_ssm[:, :t].reshape(n, D_INNER)
    y = _finish(o, y_ssm, ga, gb, x2d, p, w, tm)
    new_conv = jnp.concatenate([conv_state, xbc3], axis=1)[:, -(CONV_W - 1):]
    return (y.reshape(b, t, D_MODEL), kf.reshape(b, t, N_HEADS, 2, HEAD_DIM),
            vf.reshape(b, t, N_HEADS, V_DIM), new_conv, new_ssm)


def kernel(x_prompt, x_sample, cache_k, cache_v, state_conv, state_ssm, page_table, norm_mix_g, w_in, q_norm_g, k_norm_g, lambda_q1, lambda_k1, lambda_q2, lambda_k2, subln_g, conv_w, conv_b, dt_bias, a_log, d_skip, ssm_norm_g, w_branch_a, w_branch_b, w_out, norm_mlp_g, w_up, w_down):
    depth = w_in.shape[0]
    yp, ys = x_prompt, x_sample
    outs = [[] for _ in range(8)]
    for l in range(depth):
        p = {
            'norm_mix_g': norm_mix_g[l], 'w_in': w_in[l], 'q_norm_g': q_norm_g[l], 'k_norm_g': k_norm_g[l],
            'lambda_q1': lambda_q1[l], 'lambda_k1': lambda_k1[l], 'lambda_q2': lambda_q2[l], 'lambda_k2': lambda_k2[l],
            'subln_g': subln_g[l], 'conv_w': conv_w[l], 'conv_b': conv_b[l], 'dt_bias': dt_bias[l],
            'a_log': a_log[l], 'd_skip': d_skip[l], 'ssm_norm_g': ssm_norm_g[l],
            'w_branch_a': w_branch_a[l], 'w_branch_b': w_branch_b[l], 'w_out': w_out[l],
            'norm_mlp_g': norm_mlp_g[l], 'w_up': w_up[l], 'w_down': w_down[l],
        }
        lam_init = 0.8 - 0.6 * math.exp(-0.3 * l)
        lam = _lam(p, lam_init)
        w = _pack_weights(p)
        yp, kp, vp, cp, sp = _prompt_layer(yp, p, w, lam, lam_init)
        ys, ks, vs, cs, ss = _sample_layer(ys, p, w, lam, lam_init, cache_k, cache_v, page_table, l,
                                           state_conv[l], state_ssm[l])
        for lst, val in zip(outs, (kp, vp, cp, sp, ks, vs, cs, ss)):
            lst.append(val)
    return (yp, ys) + tuple(jnp.stack(o) for o in outs)
```

```python
import functools
import math

import numpy as np
import jax
import jax.numpy as jnp
from jax import lax
from jax.experimental import pallas as pl
from jax.experimental.pallas import tpu as pltpu

F32 = jnp.float32
BF16 = jnp.bfloat16

D_MODEL = 1024
N_HEADS = 8
HEAD_DIM = 64
V_DIM = 2 * HEAD_DIM
ATT_WIDTH = N_HEADS * V_DIM
ROPE_THETA = 10000.0
D_INNER = 2 * D_MODEL
SSM_HEAD_DIM = 64
N_SSM_HEADS = D_INNER // SSM_HEAD_DIM
N_GROUPS = 4
HEADS_PER_GROUP = N_SSM_HEADS // N_GROUPS
GROUP_WIDTH = D_INNER // N_GROUPS
D_STATE = 128
CONV_W = 4
CONV_DIM = D_INNER + 2 * N_GROUPS * D_STATE
SSD_CHUNK = 128
D_FF = 4 * D_MODEL
PAGE_SIZE = 128
EPS = 1e-6
LOG2E = 1.4426950408889634

LANES = 128
SUBLANES = 8
VMEM_LIMIT_BYTES = 56 * 1024 * 1024

NEG_BIG = -1e30

_SECTION_WIDTHS = (ATT_WIDTH, ATT_WIDTH, ATT_WIDTH, D_INNER, CONV_DIM, D_MODEL, D_MODEL)


def _section_starts(tn):
    return tuple(int(s) // tn for s in np.cumsum((0,) + _SECTION_WIDTHS))

PAGES_PER_STEP = 8
DEC_ROWS = 8
QROWS = 2 * DEC_ROWS


def _cparams(sem):
    return pltpu.CompilerParams(dimension_semantics=sem, vmem_limit_bytes=VMEM_LIMIT_BYTES)


def _dot(a, b):
    return jnp.dot(a, b, preferred_element_type=F32)


def _dot_nt(a, b):
    return lax.dot_general(a, b, (((1,), (1,)), ((), ())), preferred_element_type=F32)


def _exp2(x):
    return jnp.exp2(x)


def _rope_norm(acc, gain, cos, sin, bsel):
    lane = lax.broadcasted_iota(jnp.int32, (1, LANES), 1)
    first_half = (lane % HEAD_DIM) < (HEAD_DIM // 2)
    outs = []
    for hb in range(acc.shape[1] // LANES):
        xh = acc[:, hb * LANES:(hb + 1) * LANES]
        ss = _dot((xh * xh).astype(BF16), bsel)
        y = xh * lax.rsqrt(ss * (1.0 / HEAD_DIM) + EPS) * gain
        partner = jnp.where(first_half,
                            pltpu.roll(y, LANES - HEAD_DIM // 2, 1),
                            pltpu.roll(y, HEAD_DIM // 2, 1))
        outs.append(y * cos + partner * sin)
    return jnp.concatenate(outs, axis=-1)


def _in_proj_kernel(x_ref, g_ref, wf_ref, wg_ref, wdt_ref, qg_ref, kg_ref, cos_ref, sin_ref, bsel_ref,
                    q_ref, k_ref, v_ref, z_ref, xbc_ref, xtail_ref, dt_ref, ga_ref, gb_ref,
                    h_ref, *, starts, transpose_k):
    J_Q, J_K, J_V, J_Z, J_X, J_GA, J_GB, _ = starts
    j = pl.program_id(1)

    @pl.when(j == 0)
    def _():
        x = x_ref[...]
        ms = jnp.mean(x * x, axis=-1, keepdims=True)
        h = x * lax.rsqrt(ms + EPS) * g_ref[...]
        h_ref[...] = h.astype(BF16)
        dt_ref[...] = _dot(h_ref[...], wdt_ref[...])

    @pl.when(j < J_GA)
    def _():
        acc = _dot(h_ref[...], wf_ref[...])

        @pl.when(j < J_K)
        def _():
            r = _rope_norm(acc, qg_ref[...], cos_ref[...], sin_ref[...], bsel_ref[...])
            q_ref[...] = r.astype(BF16)

        @pl.when((j >= J_K) & (j < J_V))
        def _():
            r = _rope_norm(acc, kg_ref[...], cos_ref[...], sin_ref[...], bsel_ref[...])
            if transpose_k:
                k_ref[0] = r.T
            else:
                k_ref[...] = r

        @pl.when((j >= J_V) & (j < J_Z))
        def _():
            v_ref[...] = acc

        @pl.when((j >= J_Z) & (j < J_X))
        def _():
            z_ref[...] = acc.astype(z_ref.dtype)

        @pl.when(j >= J_X)
        def _():
            xbc_ref[...] = acc.astype(xbc_ref.dtype)
            xtail_ref[0] = acc[acc.shape[0] - SUBLANES:, :]

    @pl.when(j >= J_GA)
    def _():
        acc = _dot(h_ref[...], wg_ref[...])

        @pl.when(j < J_GB)
        def _():
            ga_ref[...] = acc.astype(ga_ref.dtype)

        @pl.when(j >= J_GB)
        def _():
            gb_ref[...] = acc.astype(gb_ref.dtype)


def _in_proj(x, g, w_front, w_gate, w_dt, qg, kg, cos_t, sin_t, bsel, tm, tn, seq_len, act_dtype):
    n = x.shape[0]
    rope_blocks = cos_t.shape[0] // tm
    starts = _section_starts(tn)
    J_Q, J_K, J_V, J_Z, J_X, J_GA, J_GB, n_tiles = starts
    n_rows = n // tm
    grid = (n_rows, n_tiles)
    const = lambda i, j: (0, 0)

    def row_after(stop):
        return lambda i, j: jnp.where(j >= stop, jnp.minimum(i + 1, n_rows - 1), i)

    def section(start, stop):
        last = stop - start - 1
        row_of = row_after(stop)

        def index(i, j):
            moved_on = (j >= stop) & (i + 1 < n_rows)
            return row_of(i, j), jnp.where(moved_on, 0, jnp.clip(j - start, 0, last))
        return index

    row = lambda i, j: (row_after(1)(i, j), 0)
    rope_map = lambda i, j: (row_after(J_V)(i, j) % rope_blocks, 0)
    in_specs = [
        pl.BlockSpec((tm, D_MODEL), row),
        pl.BlockSpec((1, D_MODEL), const),
        pl.BlockSpec((D_MODEL, tn), lambda i, j: (0, jnp.minimum(j, J_GA - 1))),
        pl.BlockSpec((D_MODEL, tn), lambda i, j: (0, jnp.maximum(j - J_GA, 0))),
        pl.BlockSpec((D_MODEL, LANES), const),
        pl.BlockSpec((1, LANES), const),
        pl.BlockSpec((1, LANES), const),
        pl.BlockSpec((tm, LANES), rope_map),
        pl.BlockSpec((tm, LANES), rope_map),
        pl.BlockSpec((LANES, LANES), const),
    ]
    transpose_k = seq_len is not None
    tiles_per_seq = seq_len // tm if transpose_k else 1
    if transpose_k:
        k_shape = (n // seq_len, ATT_WIDTH, seq_len)
        def k_map(i, j):
            r, c = section(J_K, J_V)(i, j)
            return r // tiles_per_seq, c, r % tiles_per_seq
        k_spec = pl.BlockSpec((1, tn, tm), k_map)
    else:
        k_shape = (n, ATT_WIDTH)
        k_spec = pl.BlockSpec((tm, tn), section(J_K, J_V))
    out_shape = [
        jax.ShapeDtypeStruct((n, ATT_WIDTH), BF16),
        jax.ShapeDtypeStruct(k_shape, F32),
        jax.ShapeDtypeStruct((n, ATT_WIDTH), F32),
        jax.ShapeDtypeStruct((n, D_INNER), act_dtype),
        jax.ShapeDtypeStruct((n, CONV_DIM), act_dtype),
        jax.ShapeDtypeStruct((n // tm, SUBLANES, CONV_DIM), F32),
        jax.ShapeDtypeStruct((n, LANES), F32),
        jax.ShapeDtypeStruct((n, D_MODEL), act_dtype),
        jax.ShapeDtypeStruct((n, D_MODEL), act_dtype),
    ]
    out_specs = [
        pl.BlockSpec((tm, tn), section(J_Q, J_K)),
        k_spec,
        pl.BlockSpec((tm, tn), section(J_V, J_Z)),
        pl.BlockSpec((tm, tn), section(J_Z, J_X)),
        pl.BlockSpec((tm, tn), section(J_X, J_GA)),
        pl.BlockSpec((1, SUBLANES, tn), lambda i, j: (section(J_X, J_GA)(i, j)[0], 0,
                                                      section(J_X, J_GA)(i, j)[1])),
        pl.BlockSpec((tm, LANES), row),
        pl.BlockSpec((tm, tn), section(J_GA, J_GB)),
        pl.BlockSpec((tm, tn), section(J_GB, n_tiles)),
    ]
    return pl.pallas_call(
        functools.partial(_in_proj_kernel, starts=starts, transpose_k=transpose_k),
        out_shape=out_shape,
        grid=grid,
        in_specs=in_specs,
        out_specs=out_specs,
        scratch_shapes=[pltpu.VMEM((tm, D_MODEL), BF16)],
        compiler_params=_cparams(("arbitrary", "arbitrary")),
        name="in_proj",
    )(x, g, w_front, w_gate, w_dt, qg, kg, cos_t, sin_t, bsel)


def _subln(o, g, out_scale):
    ms = jnp.mean(o * o, axis=-1, keepdims=True)
    return o * lax.rsqrt(ms + EPS) * g * out_scale


def _prompt_attn_kernel(lam_ref, q_ref, kt_ref, v_ref, g_ref, o_ref, *, blk, out_scale):
    t = q_ref.shape[1]
    lam = lam_ref[0, 0]
    row = lax.broadcasted_iota(jnp.int32, (blk, blk), 0)
    col = lax.broadcasted_iota(jnp.int32, (blk, blk), 1)
    causal = col <= row
    kt = kt_ref[0].astype(BF16)
    v = v_ref[0].astype(BF16)
    for qi in range(t // blk):
        r0 = qi * blk
        q = q_ref[0, r0:r0 + blk, :]
        ps = []
        for c in range(2):
            cs = slice(c * HEAD_DIM, (c + 1) * HEAD_DIM)
            qc = q[:, cs]
            s_d = jnp.where(causal, _dot(qc, kt[cs, r0:r0 + blk]), NEG_BIG)
            m = jnp.max(s_d, axis=-1, keepdims=True)
            p_o = None
            if qi > 0:
                s_o = _dot(qc, kt[cs, 0:r0])
                m = jnp.maximum(m, jnp.max(s_o, axis=-1, keepdims=True))
                p_o = _exp2(s_o - m)
            p_d = _exp2(s_d - m)
            l = jnp.sum(p_d, axis=-1, keepdims=True)
            if qi > 0:
                l = l + jnp.sum(p_o, axis=-1, keepdims=True)
            ps.append((p_o, p_d, l))
        w0 = 1.0 / ps[0][2]
        w1 = lam / ps[1][2]
        o = _dot((ps[0][1] * w0 - ps[1][1] * w1).astype(BF16), v[r0:r0 + blk, :])
        if qi > 0:
            o = o + _dot((ps[0][0] * w0 - ps[1][0] * w1).astype(BF16), v[0:r0, :])
        o_ref[0, r0:r0 + blk, :] = _subln(o, g_ref[...], out_scale).astype(o_ref.dtype)


def _prompt_attention(q, kt, v, lam, subln_g, out_scale, blk):
    b, t, _ = q.shape
    kern = functools.partial(_prompt_attn_kernel, blk=blk, out_scale=out_scale)
    return pl.pallas_call(
        kern,
        out_shape=jax.ShapeDtypeStruct((b, t, ATT_WIDTH), BF16),
        grid=(b, N_HEADS),
        in_specs=[
            pl.BlockSpec(memory_space=pltpu.SMEM),
            pl.BlockSpec((1, t, V_DIM), lambda bi, h: (bi, 0, h)),
            pl.BlockSpec((1, V_DIM, t), lambda bi, h: (bi, h, 0)),
            pl.BlockSpec((1, t, V_DIM), lambda bi, h: (bi, 0, h)),
            pl.BlockSpec((1, V_DIM), lambda bi, h: (0, 0)),
        ],
        out_specs=pl.BlockSpec((1, t, V_DIM), lambda bi, h: (bi, 0, h)),
        compiler_params=_cparams(("parallel", "parallel")),
        name="prompt_attention",
    )(lam, q, kt, v, subln_g)


def _sample_attn_kernel(pt_ref, lam_ref, wq_ref, knew_ref, vnew_ref, g_ref, *rest,
                        n_valid, out_scale):
    kp_refs = rest[:PAGES_PER_STEP]
    vp_refs = rest[PAGES_PER_STEP:2 * PAGES_PER_STEP]
    o_ref = rest[2 * PAGES_PER_STEP]
    m_ref, l_ref, acc_ref = rest[2 * PAGES_PER_STEP + 1:]
    j = pl.program_id(1)

    @pl.when(j == 0)
    def _():
        m_ref[...] = jnp.full(m_ref.shape, NEG_BIG, F32)
        l_ref[...] = jnp.zeros(l_ref.shape, F32)
        acc_ref[...] = jnp.zeros(acc_ref.shape, F32)

    s = jnp.concatenate(
        [jnp.concatenate([_dot(wq_ref[0, h], kp_refs[p][0, 0, h]) for p in range(PAGES_PER_STEP)],
                         axis=-1) for h in range(N_HEADS)], axis=0)
    m = m_ref[...]
    m_new = jnp.maximum(m, jnp.max(s, axis=-1, keepdims=True))
    alpha = _exp2(m - m_new)
    pr = _exp2(s - m_new)
    l_ref[...] = alpha * l_ref[...] + jnp.sum(pr, axis=-1, keepdims=True)
    m_ref[...] = m_new
    pvs = []
    for h in range(N_HEADS):
        ph = pr[h * QROWS:(h + 1) * QROWS]
        v_rows = pl.ds(h, PAGE_SIZE, stride=N_HEADS)
        pv = _dot(ph[:, :PAGE_SIZE], vp_refs[0][0, 0, v_rows, :])
        for p in range(1, PAGES_PER_STEP):
            pv = pv + _dot(ph[:, p * PAGE_SIZE:(p + 1) * PAGE_SIZE], vp_refs[p][0, 0, v_rows, :])
        pvs.append(pv)
    acc_ref[...] = alpha * acc_ref[...] + jnp.concatenate(pvs, axis=0)

    @pl.when(j == pl.num_programs(1) - 1)
    def _():
        lam = lam_ref[0, 0]
        row_t = lax.broadcasted_iota(jnp.int32, (QROWS, DEC_ROWS), 0) % DEC_ROWS
        col_t = lax.broadcasted_iota(jnp.int32, (QROWS, DEC_ROWS), 1)
        valid = (col_t <= row_t) & (col_t < n_valid)
        outs = []
        for h in range(N_HEADS):
            rs = slice(h * QROWS, (h + 1) * QROWS)
            hs = slice(h * V_DIM, (h + 1) * V_DIM)
            sn = jnp.where(valid, _dot_nt(wq_ref[0, h], knew_ref[0][:, hs]), NEG_BIG)
            mm = m_ref[rs, :]
            mf = jnp.maximum(mm, jnp.max(sn, axis=-1, keepdims=True))
            al = _exp2(mm - mf)
            pn = _exp2(sn - mf)
            lf = al * l_ref[rs, :] + jnp.sum(pn, axis=-1, keepdims=True)
            af = al * acc_ref[rs, :] + _dot(pn, vnew_ref[0][:, hs])
            of = af / lf
            od = of[:DEC_ROWS] - lam * of[DEC_ROWS:]
            outs.append(_subln(od, g_ref[...], out_scale))
        o_ref[0] = jnp.concatenate(outs, axis=-1)


def _sample_attention(page_table, lam, wq, k_new, v_new, subln_g, cache_kt, cache_v, layer,
                      n_valid, out_scale):
    db, n_pages = page_table.shape
    n_steps = n_pages // PAGES_PER_STEP

    def page_map(p, ndim):
        return lambda b, j, pt: (layer, pt[b * n_pages + j * PAGES_PER_STEP + p]) + (0,) * (ndim - 2)

    kern = functools.partial(_sample_attn_kernel, n_valid=n_valid, out_scale=out_scale)
    grid_spec = pltpu.PrefetchScalarGridSpec(
        num_scalar_prefetch=1,
        grid=(db, n_steps),
        in_specs=[
            pl.BlockSpec(memory_space=pltpu.SMEM),
            pl.BlockSpec((1, N_HEADS, QROWS, V_DIM), lambda b, j, pt: (b, 0, 0, 0)),
            pl.BlockSpec((1, DEC_ROWS, ATT_WIDTH), lambda b, j, pt: (b, 0, 0)),
            pl.BlockSpec((1, DEC_ROWS, ATT_WIDTH), lambda b, j, pt: (b, 0, 0)),
            pl.BlockSpec((1, V_DIM), lambda b, j, pt: (0, 0)),
        ] + [pl.BlockSpec((1, 1, N_HEADS, V_DIM, PAGE_SIZE), page_map(p, 5))
             for p in range(PAGES_PER_STEP)]
          + [pl.BlockSpec((1, 1, PAGE_SIZE * N_HEADS, V_DIM), page_map(p, 4))
             for p in range(PAGES_PER_STEP)],
        out_specs=pl.BlockSpec((1, DEC_ROWS, ATT_WIDTH), lambda b, j, pt: (b, 0, 0)),
        scratch_shapes=[
            pltpu.VMEM((N_HEADS * QROWS, 1), F32),
            pltpu.VMEM((N_HEADS * QROWS, 1), F32),
            pltpu.VMEM((N_HEADS * QROWS, V_DIM), F32),
        ],
    )
    return pl.pallas_call(
        kern,
        out_shape=jax.ShapeDtypeStruct((db, DEC_ROWS, ATT_WIDTH), F32),
        grid_spec=grid_spec,
        compiler_params=_cparams(("parallel", "arbitrary")),
        name="sample_attention",
    )(page_table.reshape(-1), lam, wq, k_new, v_new, subln_g,
      *([cache_kt] * PAGES_PER_STEP), *([cache_v] * PAGES_PER_STEP))


def _split3(v):
    hi = v.astype(BF16)
    r1 = v - hi.astype(F32)
    mid = r1.astype(BF16)
    lo = (r1 - mid.astype(F32)).astype(BF16)
    return hi, mid, lo


def _softplus(x):
    return jnp.maximum(x, 0.0) + jnp.log1p(jnp.exp(-jnp.abs(x)))


def _ssd_kernel(*refs, chunk, n_valid, has_init):
    (xbc_ref, z_ref, dt_ref, dtt_ref, cinit_ref) = refs[:5]
    k = 5
    if has_init:
        sinit_ref = refs[k]
        k += 1
    (cw_ref, cb_ref, bias_ref, biast_ref, alog_ref, alogt_ref, dskip_ref, ng_ref,
     ltri_ref, utri_ref, e64_ref, eseg_ref) = refs[k:k + 12]
    y_ref, sout_ref = refs[k + 12:k + 14]
    tail_ref, st_ref = refs[k + 14:]
    c = pl.program_id(1)
    L = chunk

    @pl.when(c == 0)
    def _():
        tail_ref[...] = cinit_ref[0]
        if has_init:
            st_ref[...] = sinit_ref[0].T
        else:
            st_ref[...] = jnp.zeros(st_ref.shape, F32)

    xraw = xbc_ref[0].astype(F32)
    head =jnp.concatenate([tail_ref[...], xraw[:SUBLANES]], axis=0)
    conv = cb_ref[...]
    conv_head = cb_ref[...]
    for kk in range(CONV_W):
        back = CONV_W - 1 - kk
        wk = cw_ref[kk:kk + 1, :]
        shifted = pltpu.roll(xraw, back, 0) if back else xraw
        conv = conv + shifted * wk
        conv_head = conv_head + head[SUBLANES - back:2 * SUBLANES - back] * wk
    conv = jnp.concatenate([conv_head, conv[SUBLANES:]], axis=0) if L > SUBLANES else conv_head
    tail_ref[...] = xraw[L - SUBLANES:, :]
    xact = conv * jax.nn.sigmoid(conv)
    xs = xact[:, :D_INNER]

    nh = N_SSM_HEADS
    dt = _softplus(dt_ref[0][:, :nh] + bias_ref[...])
    dtt = _softplus(dtt_ref[0] + biast_ref[...])
    if n_valid is not None:
        tok = c * L + lax.broadcasted_iota(jnp.int32, (L, 1), 0)
        dt = jnp.where(tok < n_valid, dt, 0.0)
        tokt = c * L + lax.broadcasted_iota(jnp.int32, (1, L), 1)
        dtt = jnp.where(tokt < n_valid, dtt, 0.0)
    da = dt * (-jnp.exp(alog_ref[...]))
    dat = dtt * (-jnp.exp(alogt_ref[...]))
    ltri = ltri_ref[...]
    utri = utri_ref[...]
    a = sum(_dot(ltri, part) for part in _split3(da))
    at = sum(_dot(part, utri) for part in _split3(dat))
    a_last = a[L - 1:L, :]
    w = jnp.exp(a_last - a) * dt

    def expand(v, e_ref):
        return _dot(jnp.concatenate(_split3(v), axis=-1), e_ref[...])

    aseg = expand(a, eseg_ref)
    a64 = expand(a, e64_ref)
    w64 = expand(w, e64_ref)
    ea64 = jnp.exp(a64)
    xw = (xs * w64).astype(BF16)

    row = lax.broadcasted_iota(jnp.int32, (L, L), 0)
    col = lax.broadcasted_iota(jnp.int32, (L, L), 1)
    causal = col <= row
    lane = lax.broadcasted_iota(jnp.int32, (1, LANES), 1)
    low = lane < SSM_HEAD_DIM

    ys = []
    for g in range(N_GROUPS):
        bg = xact[:, D_INNER + g * D_STATE:D_INNER + (g + 1) * D_STATE]
        cg = xact[:, D_INNER + (N_GROUPS + g) * D_STATE:D_INNER + (N_GROUPS + g + 1) * D_STATE]
        bgb = bg.astype(BF16)
        cgb = cg.astype(BF16)
        gmat = _dot_nt(cgb, bgb)
        gs = slice(g * GROUP_WIDTH, (g + 1) * GROUP_WIDTH)
        st_g = st_ref[:, gs]
        y_off = _dot(cgb, st_g.astype(BF16)) * ea64[:, gs]
        y_diag = []
        for pr in range(HEADS_PER_GROUP // 2):
            ms = []
            for hh in range(2):
                h = g * HEADS_PER_GROUP + 2 * pr + hh
                seg = aseg[:, h * L:(h + 1) * L] - at[h:h + 1, :]
                lm = jnp.where(causal, jnp.exp(seg), 0.0)
                ms.append((gmat * lm * dtt[h:h + 1, :]).astype(BF16))
            h0 = g * HEADS_PER_GROUP + 2 * pr
            xpair = xs[:, h0 * SSM_HEAD_DIM:(h0 + 2) * SSM_HEAD_DIM]
            xstack = jnp.concatenate([jnp.where(low, xpair, 0.0), jnp.where(low, 0.0, xpair)],
                                     axis=0).astype(BF16)
            y_diag.append(_dot(jnp.concatenate(ms, axis=-1), xstack))
        ys.append(jnp.concatenate(y_diag, axis=-1) + y_off)
        cd = ea64[L - 1:L, gs]
        st_ref[:, gs] = cd * st_g + _dot(bg.T.astype(BF16), xw[:, gs])
    y = jnp.concatenate(ys, axis=-1) + dskip_ref[...] * xs
    zz = z_ref[0].astype(F32)
    y = y * (zz * jax.nn.sigmoid(zz))
    outs = []
    for g in range(N_GROUPS):
        yg = y[:, g * GROUP_WIDTH:(g + 1) * GROUP_WIDTH]
        ms = jnp.mean(yg * yg, axis=-1, keepdims=True)
        outs.append(yg * lax.rsqrt(ms + EPS) * ng_ref[:, g * GROUP_WIDTH:(g + 1) * GROUP_WIDTH])
    y_ref[0] = jnp.concatenate(outs, axis=-1).astype(y_ref.dtype)

    @pl.when(c == pl.num_programs(1) - 1)
    def _():
        sout_ref[0] = st_ref[...].T


def _ssd_constants(chunk):
    idx = np.arange(chunk)
    ltri = (idx[None, :] <= idx[:, None]).astype(np.float32)
    utri = ltri.T
    k3 = np.arange(3 * N_SSM_HEADS) % N_SSM_HEADS
    e64 = (k3[:, None] == (np.arange(D_INNER)[None, :] // SSM_HEAD_DIM)).astype(np.float32)
    eseg = (k3[:, None] == (np.arange(N_SSM_HEADS * chunk)[None, :] // chunk)).astype(np.float32)
    return tuple(jnp.asarray(m, dtype=BF16) for m in (ltri, utri, e64, eseg))


def _ssd(xbc, z, dt_raw, conv_init, ssm_init, p, n_valid, chunk):
    b, t, _ = xbc.shape
    nc = t // chunk
    has_init = ssm_init is not None
    dtt = jnp.swapaxes(dt_raw[:, :, :N_SSM_HEADS], 1, 2)
    cinit = jnp.zeros((b, SUBLANES, CONV_DIM), F32)
    cinit = cinit.at[:, SUBLANES - (CONV_W - 1):, :].set(conv_init)
    consts = _ssd_constants(chunk)
    bmap = lambda bi, c: (bi, c, 0)
    const2 = lambda bi, c: (0, 0)
    args = [xbc, z, dt_raw, dtt, cinit]
    in_specs = [
        pl.BlockSpec((1, chunk, CONV_DIM), bmap),
        pl.BlockSpec((1, chunk, D_INNER), bmap),
        pl.BlockSpec((1, chunk, LANES), bmap),
        pl.BlockSpec((1, N_SSM_HEADS, chunk), lambda bi, c: (bi, 0, c)),
        pl.BlockSpec((1, SUBLANES, CONV_DIM), lambda bi, c: (bi, 0, 0)),
    ]
    if has_init:
        args.append(ssm_init.reshape(b, D_INNER, D_STATE))
        in_specs.append(pl.BlockSpec((1, D_INNER, D_STATE), lambda bi, c: (bi, 0, 0)))
    params = [p['conv_w'], p['conv_b'][None, :], p['dt_bias'][None, :], p['dt_bias'][:, None],
              p['a_log'][None, :], p['a_log'][:, None],
              jnp.repeat(p['d_skip'], SSM_HEAD_DIM)[None, :], p['ssm_norm_g'][None, :]]
    args += params + list(consts)
    in_specs += [pl.BlockSpec(a.shape, const2) for a in params + list(consts)]
    kern = functools.partial(_ssd_kernel, chunk=chunk, n_valid=n_valid, has_init=has_init)
    y, s_out = pl.pallas_call(
        kern,
        out_shape=[jax.ShapeDtypeStruct((b, t, D_INNER), BF16),
                   jax.ShapeDtypeStruct((b, D_INNER, D_STATE), F32)],
        grid=(b, nc),
        in_specs=in_specs,
        out_specs=[pl.BlockSpec((1, chunk, D_INNER), bmap),
                   pl.BlockSpec((1, D_INNER, D_STATE), lambda bi, c: (bi, 0, 0))],
        scratch_shapes=[pltpu.VMEM((SUBLANES, CONV_DIM), F32),
                        pltpu.VMEM((D_STATE, D_INNER), F32)],
        compiler_params=_cparams(("parallel", "arbitrary")),
        name="ssd",
    )(*args)
    return y, s_out.reshape(b, N_SSM_HEADS, SSM_HEAD_DIM, D_STATE)


def _merge_kernel(o_ref, y_ref, ga_ref, gb_ref, x_ref, wa_ref, wb_ref, wo_ref, out_ref):
    ba = _dot(o_ref[...], wa_ref[...])
    bb = _dot(y_ref[...], wb_ref[...])
    mix = (jax.nn.sigmoid(ga_ref[...].astype(F32)) * ba
           + jax.nn.sigmoid(gb_ref[...].astype(F32)) * bb)
    out_ref[...] = x_ref[...] + _dot(mix.astype(BF16), wo_ref[...])


def _merge(o, y, ga, gb, x, wa, wb, wo, tm):
    n = x.shape[0]
    row = lambda i: (i, 0)
    const = lambda i: (0, 0)
    return pl.pallas_call(
        _merge_kernel,
        out_shape=jax.ShapeDtypeStruct((n, D_MODEL), F32),
        grid=(n // tm,),
        in_specs=[
            pl.BlockSpec((tm, ATT_WIDTH), row),
            pl.BlockSpec((tm, D_INNER), row),
            pl.BlockSpec((tm, D_MODEL), row),
            pl.BlockSpec((tm, D_MODEL), row),
            pl.BlockSpec((tm, D_MODEL), row),
            pl.BlockSpec((ATT_WIDTH, D_MODEL), const),
            pl.BlockSpec((D_INNER, D_MODEL), const),
            pl.BlockSpec((D_MODEL, D_MODEL), const),
        ],
        out_specs=pl.BlockSpec((tm, D_MODEL), row),
        compiler_params=_cparams(("parallel",)),
        name="merge",
    )(o, y, ga, gb, x, wa, wb, wo)


def _mlp_kernel(x_ref, g_ref, wu_ref, wd_ref, out_ref, h_ref, acc_ref):
    k = pl.program_id(1)

    @pl.when(k == 0)
    def _():
        x = x_ref[...]
        ms = jnp.mean(x * x, axis=-1, keepdims=True)
        h_ref[...] = (x * lax.rsqrt(ms + EPS) * g_ref[...]).astype(BF16)
        acc_ref[...] = x

    u = _dot(h_ref[...], wu_ref[...])
    r = jnp.square(jnp.maximum(u, 0.0))
    acc_ref[...] += _dot(r.astype(BF16), wd_ref[...])

    @pl.when(k == pl.num_programs(1) - 1)
    def _():
        out_ref[...] = acc_ref[...]


def _mlp(x, g, wu, wd, tm, tk):
    n = x.shape[0]
    return pl.pallas_call(
        _mlp_kernel,
        out_shape=jax.ShapeDtypeStruct((n, D_MODEL), F32),
        grid=(n // tm, D_FF // tk),
        in_specs=[
            pl.BlockSpec((tm, D_MODEL), lambda i, k: (i, 0)),
            pl.BlockSpec((1, D_MODEL), lambda i, k: (0, 0)),
            pl.BlockSpec((D_MODEL, tk), lambda i, k: (0, k)),
            pl.BlockSpec((tk, D_MODEL), lambda i, k: (k, 0)),
        ],
        out_specs=pl.BlockSpec((tm, D_MODEL), lambda i, k: (i, 0)),
        scratch_shapes=[pltpu.VMEM((tm, D_MODEL), BF16), pltpu.VMEM((tm, D_MODEL), F32)],
        compiler_params=_cparams(("parallel", "arbitrary")),
        name="mlp",
    )(x, g, wu, wd)


def _rope_tables(pos):
    half = HEAD_DIM // 2
    inv = ROPE_THETA ** (-jnp.arange(half, dtype=F32) / half)
    ang = pos.astype(F32)[:, None] * inv[None, :]
    cos, sin = jnp.cos(ang), jnp.sin(ang)
    cos128 = jnp.tile(cos, (1, LANES // half))
    sin128 = jnp.tile(jnp.concatenate([-sin, sin], axis=-1), (1, LANES // HEAD_DIM))
    return cos128, sin128


def _pack_weights(p):
    dt0 = sum(_SECTION_WIDTHS[:5])
    w_front = p['w_in'].astype(BF16)
    w_gate = w_front[:, dt0 + N_SSM_HEADS:]
    w_dt = jnp.pad(p['w_in'][:, dt0:dt0 + N_SSM_HEADS],
                   ((0, 0), (0, LANES - N_SSM_HEADS))).astype(BF16)
    return dict(
        w_front=w_front, w_gate=w_gate, w_dt=w_dt,
        wa=p['w_branch_a'].astype(BF16), wb=p['w_branch_b'].astype(BF16),
        wo=p['w_out'].astype(BF16), wu=p['w_up'].astype(BF16), wd=p['w_down'].astype(BF16),
    )


def _mixer_common(x2d, pos_tables, p, w, tm, tn, seq_len, act_dtype):
    idx = np.arange(LANES)
    bsel = jnp.asarray((idx[:, None] // HEAD_DIM == idx[None, :] // HEAD_DIM).astype(np.float32),
                       dtype=BF16)
    q_scale = (HEAD_DIM ** -0.5) * LOG2E
    qg = (jnp.tile(p['q_norm_g'], LANES // HEAD_DIM) * q_scale)[None, :]
    kg = jnp.tile(p['k_norm_g'], LANES // HEAD_DIM)[None, :]
    cos_t, sin_t = pos_tables
    return _in_proj(x2d, p['norm_mix_g'][None, :], w['w_front'], w['w_gate'], w['w_dt'], qg, kg,
                    cos_t, sin_t,
                    bsel, tm, tn, seq_len, act_dtype)


def _finish(o, y_ssm, ga, gb, x2d, p, w, tm):
    x1 = _merge(o, y_ssm, ga, gb, x2d, w['wa'], w['wb'], w['wo'], tm)
    return _mlp(x1, p['norm_mlp_g'][None, :], w['wu'], w['wd'], tm, 1024)


def _lam(p, lam_init):
    lam = (jnp.exp(jnp.sum(p['lambda_q1'].astype(F32) * p['lambda_k1'].astype(F32)))
           - jnp.exp(jnp.sum(p['lambda_q2'].astype(F32) * p['lambda_k2'].astype(F32))) + lam_init)
    return lam.reshape(1, 1).astype(F32)


def _prompt_layer(x, p, w, lam, lam_init):
    b, t, _ = x.shape
    n = b * t
    tm = min(512, t)
    x2d = x.reshape(n, D_MODEL)
    tables = _rope_tables(jnp.arange(t))
    assert t >= SUBLANES
    q, ktf, vf, z, xbc, xtail, dt, ga, gb = _mixer_common(x2d, tables, p, w, min(1024, t), 512, t,
                                                          BF16)
    blk = min(512, t)
    o = _prompt_attention(q.reshape(b, t, -1), ktf, vf.reshape(b, t, -1), lam,
                          p['subln_g'][None, :], 1.0 - lam_init, blk)
    conv0 = jnp.zeros((b, CONV_W - 1, CONV_DIM), F32)
    y_ssm, new_ssm = _ssd(xbc.reshape(b, t, CONV_DIM), z.reshape(b, t, D_INNER),
                          dt.reshape(b, t, LANES), conv0, None, p, None, SSD_CHUNK)
    y = _finish(o.reshape(n, -1), y_ssm.reshape(n, -1), ga, gb, x2d, p, w, tm)
    new_conv = xtail.reshape(b, -1, SUBLANES, CONV_DIM)[:, -1, SUBLANES - (CONV_W - 1):]
    k_out = jnp.transpose(ktf.reshape(b, N_HEADS, 2, HEAD_DIM, t), (0, 4, 1, 2, 3))
    return (y.reshape(b, t, D_MODEL), k_out, vf.reshape(b, t, N_HEADS, V_DIM), new_conv, new_ssm)


def _sample_layer(x, p, w, lam, lam_init, cache_k, cache_v, page_table, layer, conv_state,
                  ssm_state):
    b, t, _ = x.shape
    assert t <= DEC_ROWS
    n = b * t
    tm = n
    past = page_table.shape[1] * PAGE_SIZE
    x2d = x.reshape(n, D_MODEL)
    cos_t, sin_t = _rope_tables(past + jnp.arange(t))
    tables = (jnp.tile(cos_t, (b, 1)), jnp.tile(sin_t, (b, 1)))
    q, kf, vf, z, xbc, _, dt, ga, gb = _mixer_common(x2d, tables, p, w, tm, 1024, None, F32)

    q5 = q.reshape(b, t, N_HEADS, 2, HEAD_DIM)
    q5 = jnp.pad(q5, ((0, 0), (0, DEC_ROWS - t), (0, 0), (0, 0), (0, 0)))
    eye_c = jnp.eye(2, dtype=BF16)
    wq = jnp.einsum('bthcd,cC->bhctCd', q5, eye_c).reshape(b, N_HEADS, QROWS, V_DIM).astype(F32)
    pad_t = lambda a: jnp.pad(a.reshape(b, t, -1), ((0, 0), (0, DEC_ROWS - t), (0, 0)))
    ckt = jnp.transpose(cache_k, (0, 1, 3, 4, 5, 2)).reshape(
        cache_k.shape[0], cache_k.shape[1], N_HEADS, V_DIM, PAGE_SIZE)
    cv = cache_v.reshape(cache_v.shape[0], cache_v.shape[1], PAGE_SIZE * N_HEADS, V_DIM)
    o = _sample_attention(page_table, lam, wq, pad_t(kf), pad_t(vf), p['subln_g'][None, :], ckt,
                          cv, layer, t, 1.0 - lam_init)
    o = o[:, :t].reshape(n, ATT_WIDTH).astype(BF16)

    tpad = DEC_ROWS
    pad_c = lambda a: jnp.pad(a.reshape(b, t, -1), ((0, 0), (0, tpad - t), (0, 0)))
    xbc3 = xbc.reshape(b, t, CONV_DIM)
    y_ssm, new_ssm = _ssd(pad_c(xbc), pad_c(z), pad_c(dt), conv_state, ssm_state, p, t, tpad)
    y_ssm = y_ssm[:, :t].reshape(n, D_INNER)
    y = _finish(o, y_ssm, ga, gb, x2d, p, w, tm)
    new_conv = jnp.concatenate([conv_state, xbc3], axis=1)[:, -(CONV_W - 1):]
    return (y.reshape(b, t, D_MODEL), kf.reshape(b, t, N_HEADS, 2, HEAD_DIM),
            vf.reshape(b, t, N_HEADS, V_DIM), new_conv, new_ssm)


def kernel(x_prompt, x_sample, cache_k, cache_v, state_conv, state_ssm, page_table, norm_mix_g, w_in, q_norm_g, k_norm_g, lambda_q1, lambda_k1, lambda_q2, lambda_k2, subln_g, conv_w, conv_b, dt_bias, a_log, d_skip, ssm_norm_g, w_branch_a, w_branch_b, w_out, norm_mlp_g, w_up, w_down):
    depth = w_in.shape[0]
    yp, ys = x_prompt, x_sample
    outs = [[] for _ in range(8)]
    for l in range(depth):
        p = {
            'norm_mix_g': norm_mix_g[l], 'w_in': w_in[l], 'q_norm_g': q_norm_g[l], 'k_norm_g': k_norm_g[l],
            'lambda_q1': lambda_q1[l], 'lambda_k1': lambda_k1[l], 'lambda_q2': lambda_q2[l], 'lambda_k2': lambda_k2[l],
            'subln_g': subln_g[l], 'conv_w': conv_w[l], 'conv_b': conv_b[l], 'dt_bias': dt_bias[l],
            'a_log': a_log[l], 'd_skip': d_skip[l], 'ssm_norm_g': ssm_norm_g[l],
            'w_branch_a': w_branch_a[l], 'w_branch_b': w_branch_b[l], 'w_out': w_out[l],
            'norm_mlp_g': norm_mlp_g[l], 'w_up': w_up[l], 'w_down': w_down[l],
        }
        lam_init = 0.8 - 0.6 * math.exp(-0.3 * l)
        lam = _lam(p, lam_init)
        w = _pack_weights(p)
        yp, kp, vp, cp, sp = _prompt_layer(yp, p, w, lam, lam_init)
        ys, ks, vs, cs, ss = _sample_layer(ys, p, w, lam, lam_init, cache_k, cache_v, page_table, l,
                                           state_conv[l], state_ssm[l])
        for lst, val in zip(outs, (kp, vp, cp, sp, ks, vs, cs, ss)):
            lst.append(val)
    return (yp, ys) + tuple(jnp.stack(o) for o in outs)
```

```python
import functools
import math

import numpy as np
import jax
import jax.numpy as jnp
from jax import lax
from jax.experimental import pallas as pl
from jax.experimental.pallas import tpu as pltpu

F32 = jnp.float32
BF16 = jnp.bfloat16

D_MODEL = 1024
N_HEADS = 8
HEAD_DIM = 64
V_DIM = 2 * HEAD_DIM
ATT_WIDTH = N_HEADS * V_DIM
ROPE_THETA = 10000.0
D_INNER = 2 * D_MODEL
SSM_HEAD_DIM = 64
N_SSM_HEADS = D_INNER // SSM_HEAD_DIM
N_GROUPS = 4
HEADS_PER_GROUP = N_SSM_HEADS // N_GROUPS
GROUP_WIDTH = D_INNER // N_GROUPS
D_STATE = 128
CONV_W = 4
CONV_DIM = D_INNER + 2 * N_GROUPS * D_STATE
SSD_CHUNK = 128
D_FF = 4 * D_MODEL
PAGE_SIZE = 128
EPS = 1e-6
LOG2E = 1.4426950408889634

LANES = 128
SUBLANES = 8
VMEM_LIMIT_BYTES = 56 * 1024 * 1024

NEG_BIG = -1e30

_SECTION_WIDTHS = (ATT_WIDTH, ATT_WIDTH, ATT_WIDTH, D_INNER, CONV_DIM, D_MODEL, D_MODEL)
MAIN_PROJ_TILE = 1024

PAGES_PER_STEP = 8
DEC_ROWS = 8
QROWS = 2 * DEC_ROWS


def _cparams(sem):
    return pltpu.CompilerParams(dimension_semantics=sem, vmem_limit_bytes=VMEM_LIMIT_BYTES)


def _dot(a, b):
    return jnp.dot(a, b, preferred_element_type=F32)


def _dot_nt(a, b):
    return lax.dot_general(a, b, (((1,), (1,)), ((), ())), preferred_element_type=F32)


def _exp2(x):
    return jnp.exp2(x)


def _rope_norm(acc, gain, cos, sin, bsel):
    lane = lax.broadcasted_iota(jnp.int32, (1, LANES), 1)
    first_half = (lane % HEAD_DIM) < (HEAD_DIM // 2)
    outs = []
    for hb in range(acc.shape[1] // LANES):
        xh = acc[:, hb * LANES:(hb + 1) * LANES]
        ss = _dot((xh * xh).astype(BF16), bsel)
        y = xh * lax.rsqrt(ss * (1.0 / HEAD_DIM) + EPS) * gain
        partner = jnp.where(first_half,
                            pltpu.roll(y, LANES - HEAD_DIM // 2, 1),
                            pltpu.roll(y, HEAD_DIM // 2, 1))
        outs.append(y * cos + partner * sin)
    return jnp.concatenate(outs, axis=-1)


def _qk_proj_kernel(x_ref, g_ref, w_ref, qg_ref, kg_ref, cos_ref, sin_ref, bsel_ref,
                    q_ref, k_ref, h_ref, *, n_q, transpose_k):
    j = pl.program_id(1)

    @pl.when(j == 0)
    def _():
        x = x_ref[...]
        ms = jnp.mean(x * x, axis=-1, keepdims=True)
        h_ref[...] = (x * lax.rsqrt(ms + EPS) * g_ref[...]).astype(BF16)

    acc = _dot(h_ref[...], w_ref[...])

    @pl.when(j < n_q)
    def _():
        r = _rope_norm(acc, qg_ref[...], cos_ref[...], sin_ref[...], bsel_ref[...])
        q_ref[...] = r.astype(BF16)

    @pl.when(j >= n_q)
    def _():
        r = _rope_norm(acc, kg_ref[...], cos_ref[...], sin_ref[...], bsel_ref[...])
        if transpose_k:
            k_ref[0] = r.T
        else:
            k_ref[...] = r


def _main_proj_kernel(h_ref, wf_ref, wg_ref, wdt_ref,
                      v_ref, z_ref, xbc_ref, xtail_ref, dt_ref, ga_ref, gb_ref, *, starts):
    J_V, J_Z, J_X, J_GA, J_GB, _ = starts
    j = pl.program_id(1)

    @pl.when(j == 0)
    def _():
        dt_ref[...] = _dot(h_ref[...], wdt_ref[...])

    @pl.when(j < J_GA)
    def _():
        acc = _dot(h_ref[...], wf_ref[...])

        @pl.when(j < J_Z)
        def _():
            v_ref[...] = acc

        @pl.when((j >= J_Z) & (j < J_X))
        def _():
            z_ref[...] = acc.astype(z_ref.dtype)

        @pl.when(j >= J_X)
        def _():
            xbc_ref[...] = acc.astype(xbc_ref.dtype)
            xtail_ref[0] = acc[acc.shape[0] - SUBLANES:, :]

    @pl.when(j >= J_GA)
    def _():
        acc = _dot(h_ref[...], wg_ref[...])

        @pl.when(j < J_GB)
        def _():
            ga_ref[...] = acc.astype(ga_ref.dtype)

        @pl.when(j >= J_GB)
        def _():
            gb_ref[...] = acc.astype(gb_ref.dtype)


def _block_maps(n_rows):
    def row_after(stop):
        if n_rows == 1:
            return lambda i, j: i
        return lambda i, j: jnp.where(j >= stop, jnp.minimum(i + 1, n_rows - 1), i)

    def section(start, stop):
        last = stop - start - 1
        row_of = row_after(stop)

        def index(i, j):
            col = jnp.clip(j - start, 0, last)
            if n_rows == 1:
                return i, col
            moved_on = (j >= stop) & (i + 1 < n_rows)
            return row_of(i, j), jnp.where(moved_on, 0, col)
        return index

    return row_after, section


def _in_proj(x, g, w_front, w_gate, w_dt, qg, kg, cos_t, sin_t, bsel, tm, tn_qk, seq_len, act_dtype):
    n = x.shape[0]
    n_rows = n // tm
    rope_blocks = cos_t.shape[0] // tm
    row_after, section = _block_maps(n_rows)
    const = lambda i, j: (0, 0)

    n_q = ATT_WIDTH // tn_qk
    n_qk = 2 * n_q
    transpose_k = seq_len is not None
    tiles_per_seq = seq_len // tm if transpose_k else 1
    if transpose_k:
        k_shape = (n // seq_len, ATT_WIDTH, seq_len)

        def k_map(i, j):
            r, c = section(n_q, n_qk)(i, j)
            return r // tiles_per_seq, c, r % tiles_per_seq
        k_spec = pl.BlockSpec((1, tn_qk, tm), k_map)
    else:
        k_shape = (n, ATT_WIDTH)
        k_spec = pl.BlockSpec((tm, tn_qk), section(n_q, n_qk))
    x_map = lambda i, j: (row_after(1)(i, j), 0)
    q, k, h = pl.pallas_call(
        functools.partial(_qk_proj_kernel, n_q=n_q, transpose_k=transpose_k),
        out_shape=[jax.ShapeDtypeStruct((n, ATT_WIDTH), BF16),
                   jax.ShapeDtypeStruct(k_shape, F32),
                   jax.ShapeDtypeStruct((n, D_MODEL), BF16)],
        grid=(n_rows, n_qk),
        in_specs=[
            pl.BlockSpec((tm, D_MODEL), x_map),
            pl.BlockSpec((1, D_MODEL), const),
            pl.BlockSpec((D_MODEL, tn_qk), lambda i, j: (0, j)),
            pl.BlockSpec((1, LANES), const),
            pl.BlockSpec((1, LANES), const),
            pl.BlockSpec((tm, LANES), lambda i, j: (i % rope_blocks, 0)),
            pl.BlockSpec((tm, LANES), lambda i, j: (i % rope_blocks, 0)),
            pl.BlockSpec((LANES, LANES), const),
        ],
        out_specs=[pl.BlockSpec((tm, tn_qk), section(0, n_q)),
                   k_spec,
                   pl.BlockSpec((tm, D_MODEL), lambda i, j: (i, 0))],
        compiler_params=_cparams(("arbitrary", "arbitrary")),
        name="qk_proj",
    )(x, g, w_front, qg, kg, cos_t, sin_t, bsel)

    tn = MAIN_PROJ_TILE
    starts = tuple(int(c) // tn for c in np.cumsum((0,) + _SECTION_WIDTHS[2:]))
    J_V, J_Z, J_X, J_GA, J_GB, n_tiles = starts
    qk_tiles = (2 * ATT_WIDTH) // tn
    v, z, xbc, xtail, dt, ga, gb = pl.pallas_call(
        functools.partial(_main_proj_kernel, starts=starts),
        out_shape=[
            jax.ShapeDtypeStruct((n, ATT_WIDTH), F32),
            jax.ShapeDtypeStruct((n, D_INNER), act_dtype),
            jax.ShapeDtypeStruct((n, CONV_DIM), act_dtype),
            jax.ShapeDtypeStruct((n_rows, SUBLANES, CONV_DIM), F32),
            jax.ShapeDtypeStruct((n, LANES), F32),
            jax.ShapeDtypeStruct((n, D_MODEL), act_dtype),
            jax.ShapeDtypeStruct((n, D_MODEL), act_dtype),
        ],
        grid=(n_rows, n_tiles),
        in_specs=[
            pl.BlockSpec((tm, D_MODEL), lambda i, j: (i, 0)),
            pl.BlockSpec((D_MODEL, tn), lambda i, j: (0, qk_tiles + jnp.minimum(j, J_GA - 1))),
            pl.BlockSpec((D_MODEL, tn), lambda i, j: (0, jnp.maximum(j - J_GA, 0))),
            pl.BlockSpec((D_MODEL, LANES), const),
        ],
        out_specs=[
            pl.BlockSpec((tm, tn), section(J_V, J_Z)),
            pl.BlockSpec((tm, tn), section(J_Z, J_X)),
            pl.BlockSpec((tm, tn), section(J_X, J_GA)),
            pl.BlockSpec((1, SUBLANES, tn), lambda i, j: (section(J_X, J_GA)(i, j)[0], 0,
                                                          section(J_X, J_GA)(i, j)[1])),
            pl.BlockSpec((tm, LANES), lambda i, j: (row_after(1)(i, j), 0)),
            pl.BlockSpec((tm, tn), section(J_GA, J_GB)),
            pl.BlockSpec((tm, tn), section(J_GB, n_tiles)),
        ],
        compiler_params=_cparams(("arbitrary", "arbitrary")),
        name="main_proj",
    )(h, w_front, w_gate, w_dt)
    return q, k, v, z, xbc, xtail, dt, ga, gb


def _subln(o, g, out_scale):
    ms = jnp.mean(o * o, axis=-1, keepdims=True)
    return o * lax.rsqrt(ms + EPS) * g * out_scale


def _prompt_attn_kernel(lam_ref, q_ref, kt_ref, v_ref, g_ref, o_ref, *, blk, out_scale):
    t = q_ref.shape[1]
    lam = lam_ref[0, 0]
    row = lax.broadcasted_iota(jnp.int32, (blk, blk), 0)
    col = lax.broadcasted_iota(jnp.int32, (blk, blk), 1)
    causal = col <= row
    kt = kt_ref[0].astype(BF16)
    v = v_ref[0].astype(BF16)
    for qi in range(t // blk):
        r0 = qi * blk
        q = q_ref[0, r0:r0 + blk, :]
        ps = []
        for c in range(2):
            cs = slice(c * HEAD_DIM, (c + 1) * HEAD_DIM)
            qc = q[:, cs]
            s_d = jnp.where(causal, _dot(qc, kt[cs, r0:r0 + blk]), NEG_BIG)
            m = jnp.max(s_d, axis=-1, keepdims=True)
            p_o = None
            if qi > 0:
                s_o = _dot(qc, kt[cs, 0:r0])
                m = jnp.maximum(m, jnp.max(s_o, axis=-1, keepdims=True))
                p_o = _exp2(s_o - m)
            p_d = _exp2(s_d - m)
            l = jnp.sum(p_d, axis=-1, keepdims=True)
            if qi > 0:
                l = l + jnp.sum(p_o, axis=-1, keepdims=True)
            ps.append((p_o, p_d, l))
        w0 = 1.0 / ps[0][2]
        w1 = lam / ps[1][2]
        o = _dot((ps[0][1] * w0 - ps[1][1] * w1).astype(BF16), v[r0:r0 + blk, :])
        if qi > 0:
            o = o + _dot((ps[0][0] * w0 - ps[1][0] * w1).astype(BF16), v[0:r0, :])
        o_ref[0, r0:r0 + blk, :] = _subln(o, g_ref[...], out_scale).astype(o_ref.dtype)


def _prompt_attention(q, kt, v, lam, subln_g, out_scale, blk):
    b, t, _ = q.shape
    kern = functools.partial(_prompt_attn_kernel, blk=blk, out_scale=out_scale)
    return pl.pallas_call(
        kern,
        out_shape=jax.ShapeDtypeStruct((b, t, ATT_WIDTH), BF16),
        grid=(b, N_HEADS),
        in_specs=[
            pl.BlockSpec(memory_space=pltpu.SMEM),
            pl.BlockSpec((1, t, V_DIM), lambda bi, h: (bi, 0, h)),
            pl.BlockSpec((1, V_DIM, t), lambda bi, h: (bi, h, 0)),
            pl.BlockSpec((1, t, V_DIM), lambda bi, h: (bi, 0, h)),
            pl.BlockSpec((1, V_DIM), lambda bi, h: (0, 0)),
        ],
        out_specs=pl.BlockSpec((1, t, V_DIM), lambda bi, h: (bi, 0, h)),
        compiler_params=_cparams(("parallel", "parallel")),
        name="prompt_attention",
    )(lam, q, kt, v, subln_g)


def _sample_attn_kernel(pt_ref, lam_ref, wq_ref, knew_ref, vnew_ref, g_ref, *rest,
                        n_valid, out_scale):
    kp_refs = rest[:PAGES_PER_STEP]
    vp_refs = rest[PAGES_PER_STEP:2 * PAGES_PER_STEP]
    o_ref = rest[2 * PAGES_PER_STEP]
    m_ref, l_ref, acc_ref = rest[2 * PAGES_PER_STEP + 1:]
    j = pl.program_id(1)

    @pl.when(j == 0)
    def _():
        m_ref[...] = jnp.full(m_ref.shape, NEG_BIG, F32)
        l_ref[...] = jnp.zeros(l_ref.shape, F32)
        acc_ref[...] = jnp.zeros(acc_ref.shape, F32)

    s = jnp.concatenate(
        [jnp.concatenate([_dot(wq_ref[0, h], kp_refs[p][0, 0, h]) for p in range(PAGES_PER_STEP)],
                         axis=-1) for h in range(N_HEADS)], axis=0)
    m = m_ref[...]
    m_new = jnp.maximum(m, jnp.max(s, axis=-1, keepdims=True))
    alpha = _exp2(m - m_new)
    pr = _exp2(s - m_new)
    l_ref[...] = alpha * l_ref[...] + jnp.sum(pr, axis=-1, keepdims=True)
    m_ref[...] = m_new
    pvs = []
    for h in range(N_HEADS):
        ph = pr[h * QROWS:(h + 1) * QROWS]
        v_rows = pl.ds(h, PAGE_SIZE, stride=N_HEADS)
        pv = _dot(ph[:, :PAGE_SIZE], vp_refs[0][0, 0, v_rows, :])
        for p in range(1, PAGES_PER_STEP):
            pv = pv + _dot(ph[:, p * PAGE_SIZE:(p + 1) * PAGE_SIZE], vp_refs[p][0, 0, v_rows, :])
        pvs.append(pv)
    acc_ref[...] = alpha * acc_ref[...] + jnp.concatenate(pvs, axis=0)

    @pl.when(j == pl.num_programs(1) - 1)
    def _():
        lam = lam_ref[0, 0]
        row_t = lax.broadcasted_iota(jnp.int32, (QROWS, DEC_ROWS), 0) % DEC_ROWS
        col_t = lax.broadcasted_iota(jnp.int32, (QROWS, DEC_ROWS), 1)
        valid = (col_t <= row_t) & (col_t < n_valid)
        outs = []
        for h in range(N_HEADS):
            rs = slice(h * QROWS, (h + 1) * QROWS)
            hs = slice(h * V_DIM, (h + 1) * V_DIM)
            sn = jnp.where(valid, _dot_nt(wq_ref[0, h], knew_ref[0][:, hs]), NEG_BIG)
            mm = m_ref[rs, :]
            mf = jnp.maximum(mm, jnp.max(sn, axis=-1, keepdims=True))
            al = _exp2(mm - mf)
            pn = _exp2(sn - mf)
            lf = al * l_ref[rs, :] + jnp.sum(pn, axis=-1, keepdims=True)
            af = al * acc_ref[rs, :] + _dot(pn, vnew_ref[0][:, hs])
            of = af / lf
            od = of[:DEC_ROWS] - lam * of[DEC_ROWS:]
            outs.append(_subln(od, g_ref[...], out_scale))
        o_ref[0] = jnp.concatenate(outs, axis=-1)


def _sample_attention(page_table, lam, wq, k_new, v_new, subln_g, cache_kt, cache_v, layer,
                      n_valid, out_scale):
    db, n_pages = page_table.shape
    n_steps = n_pages // PAGES_PER_STEP

    def page_map(p, ndim):
        return lambda b, j, pt: (layer, pt[b * n_pages + j * PAGES_PER_STEP + p]) + (0,) * (ndim - 2)

    kern = functools.partial(_sample_attn_kernel, n_valid=n_valid, out_scale=out_scale)
    grid_spec = pltpu.PrefetchScalarGridSpec(
        num_scalar_prefetch=1,
        grid=(db, n_steps),
        in_specs=[
            pl.BlockSpec(memory_space=pltpu.SMEM),
            pl.BlockSpec((1, N_HEADS, QROWS, V_DIM), lambda b, j, pt: (b, 0, 0, 0)),
            pl.BlockSpec((1, DEC_ROWS, ATT_WIDTH), lambda b, j, pt: (b, 0, 0)),
            pl.BlockSpec((1, DEC_ROWS, ATT_WIDTH), lambda b, j, pt: (b, 0, 0)),
            pl.BlockSpec((1, V_DIM), lambda b, j, pt: (0, 0)),
        ] + [pl.BlockSpec((1, 1, N_HEADS, V_DIM, PAGE_SIZE), page_map(p, 5))
             for p in range(PAGES_PER_STEP)]
          + [pl.BlockSpec((1, 1, PAGE_SIZE * N_HEADS, V_DIM), page_map(p, 4))
             for p in range(PAGES_PER_STEP)],
        out_specs=pl.BlockSpec((1, DEC_ROWS, ATT_WIDTH), lambda b, j, pt: (b, 0, 0)),
        scratch_shapes=[
            pltpu.VMEM((N_HEADS * QROWS, 1), F32),
            pltpu.VMEM((N_HEADS * QROWS, 1), F32),
            pltpu.VMEM((N_HEADS * QROWS, V_DIM), F32),
        ],
    )
    return pl.pallas_call(
        kern,
        out_shape=jax.ShapeDtypeStruct((db, DEC_ROWS, ATT_WIDTH), F32),
        grid_spec=grid_spec,
        compiler_params=_cparams(("parallel", "arbitrary")),
        name="sample_attention",
    )(page_table.reshape(-1), lam, wq, k_new, v_new, subln_g,
      *([cache_kt] * PAGES_PER_STEP), *([cache_v] * PAGES_PER_STEP))


def _split3(v):
    hi = v.astype(BF16)
    r1 = v - hi.astype(F32)
    mid = r1.astype(BF16)
    lo = (r1 - mid.astype(F32)).astype(BF16)
    return hi, mid, lo


def _softplus(x):
    return jnp.maximum(x, 0.0) + jnp.log1p(jnp.exp(-jnp.abs(x)))


def _ssd_kernel(*refs, chunk, n_valid, has_init):
    (xbc_ref, z_ref, dt_ref, dtt_ref, cinit_ref) = refs[:5]
    k = 5
    if has_init:
        sinit_ref = refs[k]
        k += 1
    (cw_ref, cb_ref, bias_ref, biast_ref, alog_ref, alogt_ref, dskip_ref, ng_ref,
     ltri_ref, utri_ref, e64_ref, eseg_ref) = refs[k:k + 12]
    y_ref, sout_ref = refs[k + 12:k + 14]
    tail_ref, st_ref = refs[k + 14:]
    c = pl.program_id(1)
    L = chunk

    @pl.when(c == 0)
    def _():
        tail_ref[...] = cinit_ref[0]
        if has_init:
            st_ref[...] = sinit_ref[0].T
        else:
            st_ref[...] = jnp.zeros(st_ref.shape, F32)

    xraw = xbc_ref[0].astype(F32)
    head =jnp.concatenate([tail_ref[...], xraw[:SUBLANES]], axis=0)
    conv = cb_ref[...]
    conv_head = cb_ref[...]
    for kk in range(CONV_W):
        back = CONV_W - 1 - kk
        wk = cw_ref[kk:kk + 1, :]
        shifted = pltpu.roll(xraw, back, 0) if back else xraw
        conv = conv + shifted * wk
        conv_head = conv_head + head[SUBLANES - back:2 * SUBLANES - back] * wk
    conv = jnp.concatenate([conv_head, conv[SUBLANES:]], axis=0) if L > SUBLANES else conv_head
    tail_ref[...] = xraw[L - SUBLANES:, :]
    xact = conv * jax.nn.sigmoid(conv)
    xs = xact[:, :D_INNER]

    nh = N_SSM_HEADS
    dt = _softplus(dt_ref[0][:, :nh] + bias_ref[...])
    dtt = _softplus(dtt_ref[0] + biast_ref[...])
    if n_valid is not None:
        tok = c * L + lax.broadcasted_iota(jnp.int32, (L, 1), 0)
        dt = jnp.where(tok < n_valid, dt, 0.0)
        tokt = c * L + lax.broadcasted_iota(jnp.int32, (1, L), 1)
        dtt = jnp.where(tokt < n_valid, dtt, 0.0)
    da = dt * (-jnp.exp(alog_ref[...]))
    dat = dtt * (-jnp.exp(alogt_ref[...]))
    ltri = ltri_ref[...]
    utri = utri_ref[...]
    a = sum(_dot(ltri, part) for part in _split3(da))
    at = sum(_dot(part, utri) for part in _split3(dat))
    a_last = a[L - 1:L, :]
    w = jnp.exp(a_last - a) * dt

    def expand(v, e_ref):
        return _dot(jnp.concatenate(_split3(v), axis=-1), e_ref[...])

    aseg = expand(a, eseg_ref)
    a64 = expand(a, e64_ref)
    w64 = expand(w, e64_ref)
    ea64 = jnp.exp(a64)
    xw = (xs * w64).astype(BF16)

    row = lax.broadcasted_iota(jnp.int32, (L, L), 0)
    col = lax.broadcasted_iota(jnp.int32, (L, L), 1)
    causal = col <= row
    lane = lax.broadcasted_iota(jnp.int32, (1, LANES), 1)
    low = lane < SSM_HEAD_DIM

    ys = []
    for g in range(N_GROUPS):
        bg = xact[:, D_INNER + g * D_STATE:D_INNER + (g + 1) * D_STATE]
        cg = xact[:, D_INNER + (N_GROUPS + g) * D_STATE:D_INNER + (N_GROUPS + g + 1) * D_STATE]
        bgb = bg.astype(BF16)
        cgb = cg.astype(BF16)
        gmat = _dot_nt(cgb, bgb)
        gs = slice(g * GROUP_WIDTH, (g + 1) * GROUP_WIDTH)
        st_g = st_ref[:, gs]
        y_off = _dot(cgb, st_g.astype(BF16)) * ea64[:, gs]
        y_diag = []
        for pr in range(HEADS_PER_GROUP // 2):
            ms = []
            for hh in range(2):
                h = g * HEADS_PER_GROUP + 2 * pr + hh
                seg = aseg[:, h * L:(h + 1) * L] - at[h:h + 1, :]
                lm = jnp.where(causal, jnp.exp(seg), 0.0)
                ms.append((gmat * lm * dtt[h:h + 1, :]).astype(BF16))
            h0 = g * HEADS_PER_GROUP + 2 * pr
            xpair = xs[:, h0 * SSM_HEAD_DIM:(h0 + 2) * SSM_HEAD_DIM]
            xstack = jnp.concatenate([jnp.where(low, xpair, 0.0), jnp.where(low, 0.0, xpair)],
                                     axis=0).astype(BF16)
            y_diag.append(_dot(jnp.concatenate(ms, axis=-1), xstack))
        ys.append(jnp.concatenate(y_diag, axis=-1) + y_off)
        cd = ea64[L - 1:L, gs]
        st_ref[:, gs] = cd * st_g + _dot(bg.T.astype(BF16), xw[:, gs])
    y = jnp.concatenate(ys, axis=-1) + dskip_ref[...] * xs
    zz = z_ref[0].astype(F32)
    y = y * (zz * jax.nn.sigmoid(zz))
    outs = []
    for g in range(N_GROUPS):
        yg = y[:, g * GROUP_WIDTH:(g + 1) * GROUP_WIDTH]
        ms = jnp.mean(yg * yg, axis=-1, keepdims=True)
        outs.append(yg * lax.rsqrt(ms + EPS) * ng_ref[:, g * GROUP_WIDTH:(g + 1) * GROUP_WIDTH])
    y_ref[0] = jnp.concatenate(outs, axis=-1).astype(y_ref.dtype)

    @pl.when(c == pl.num_programs(1) - 1)
    def _():
        sout_ref[0] = st_ref[...].T


def _ssd_constants(chunk):
    idx = np.arange(chunk)
    ltri = (idx[None, :] <= idx[:, None]).astype(np.float32)
    utri = ltri.T
    k3 = np.arange(3 * N_SSM_HEADS) % N_SSM_HEADS
    e64 = (k3[:, None] == (np.arange(D_INNER)[None, :] // SSM_HEAD_DIM)).astype(np.float32)
    eseg = (k3[:, None] == (np.arange(N_SSM_HEADS * chunk)[None, :] // chunk)).astype(np.float32)
    return tuple(jnp.asarray(m, dtype=BF16) for m in (ltri, utri, e64, eseg))


def _ssd(xbc, z, dt_raw, conv_init, ssm_init, p, n_valid, chunk):
    b, t, _ = xbc.shape
    nc = t // chunk
    has_init = ssm_init is not None
    dtt = jnp.swapaxes(dt_raw[:, :, :N_SSM_HEADS], 1, 2)
    cinit = jnp.zeros((b, SUBLANES, CONV_DIM), F32)
    cinit = cinit.at[:, SUBLANES - (CONV_W - 1):, :].set(conv_init)
    consts = _ssd_constants(chunk)
    bmap = lambda bi, c: (bi, c, 0)
    const2 = lambda bi, c: (0, 0)
    args = [xbc, z, dt_raw, dtt, cinit]
    in_specs = [
        pl.BlockSpec((1, chunk, CONV_DIM), bmap),
        pl.BlockSpec((1, chunk, D_INNER), bmap),
        pl.BlockSpec((1, chunk, LANES), bmap),
        pl.BlockSpec((1, N_SSM_HEADS, chunk), lambda bi, c: (bi, 0, c)),
        pl.BlockSpec((1, SUBLANES, CONV_DIM), lambda bi, c: (bi, 0, 0)),
    ]
    if has_init:
        args.append(ssm_init.reshape(b, D_INNER, D_STATE))
        in_specs.append(pl.BlockSpec((1, D_INNER, D_STATE), lambda bi, c: (bi, 0, 0)))
    params = [p['conv_w'], p['conv_b'][None, :], p['dt_bias'][None, :], p['dt_bias'][:, None],
              p['a_log'][None, :], p['a_log'][:, None],
              jnp.repeat(p['d_skip'], SSM_HEAD_DIM)[None, :], p['ssm_norm_g'][None, :]]
    args += params + list(consts)
    in_specs += [pl.BlockSpec(a.shape, const2) for a in params + list(consts)]
    kern = functools.partial(_ssd_kernel, chunk=chunk, n_valid=n_valid, has_init=has_init)
    y, s_out = pl.pallas_call(
        kern,
        out_shape=[jax.ShapeDtypeStruct((b, t, D_INNER), BF16),
                   jax.ShapeDtypeStruct((b, D_INNER, D_STATE), F32)],
        grid=(b, nc),
        in_specs=in_specs,
        out_specs=[pl.BlockSpec((1, chunk, D_INNER), bmap),
                   pl.BlockSpec((1, D_INNER, D_STATE), lambda bi, c: (bi, 0, 0))],
        scratch_shapes=[pltpu.VMEM((SUBLANES, CONV_DIM), F32),
                        pltpu.VMEM((D_STATE, D_INNER), F32)],
        compiler_params=_cparams(("parallel", "arbitrary")),
        name="ssd",
    )(*args)
    return y, s_out.reshape(b, N_SSM_HEADS, SSM_HEAD_DIM, D_STATE)


def _merge_kernel(o_ref, y_ref, ga_ref, gb_ref, x_ref, wa_ref, wb_ref, wo_ref, out_ref):
    ba = _dot(o_ref[...], wa_ref[...])
    bb = _dot(y_ref[...], wb_ref[...])
    mix = (jax.nn.sigmoid(ga_ref[...].astype(F32)) * ba
           + jax.nn.sigmoid(gb_ref[...].astype(F32)) * bb)
    out_ref[...] = x_ref[...] + _dot(mix.astype(BF16), wo_ref[...])


def _merge(o, y, ga, gb, x, wa, wb, wo, tm):
    n = x.shape[0]
    row = lambda i: (i, 0)
    const = lambda i: (0, 0)
    return pl.pallas_call(
        _merge_kernel,
        out_shape=jax.ShapeDtypeStruct((n, D_MODEL), F32),
        grid=(n // tm,),
        in_specs=[
            pl.BlockSpec((tm, ATT_WIDTH), row),
            pl.BlockSpec((tm, D_INNER), row),
            pl.BlockSpec((tm, D_MODEL), row),
            pl.BlockSpec((tm, D_MODEL), row),
            pl.BlockSpec((tm, D_MODEL), row),
            pl.BlockSpec((ATT_WIDTH, D_MODEL), const),
            pl.BlockSpec((D_INNER, D_MODEL), const),
            pl.BlockSpec((D_MODEL, D_MODEL), const),
        ],
        out_specs=pl.BlockSpec((tm, D_MODEL), row),
        compiler_params=_cparams(("parallel",)),
        name="merge",
    )(o, y, ga, gb, x, wa, wb, wo)


def _mlp_kernel(x_ref, g_ref, wu_ref, wd_ref, out_ref, h_ref, acc_ref):
    k = pl.program_id(1)

    @pl.when(k == 0)
    def _():
        x = x_ref[...]
        ms = jnp.mean(x * x, axis=-1, keepdims=True)
        h_ref[...] = (x * lax.rsqrt(ms + EPS) * g_ref[...]).astype(BF16)
        acc_ref[...] = x

    u = _dot(h_ref[...], wu_ref[...])
    r = jnp.square(jnp.maximum(u, 0.0))
    acc_ref[...] += _dot(r.astype(BF16), wd_ref[...])

    @pl.when(k == pl.num_programs(1) - 1)
    def _():
        out_ref[...] = acc_ref[...]


def _mlp(x, g, wu, wd, tm, tk):
    n = x.shape[0]
    return pl.pallas_call(
        _mlp_kernel,
        out_shape=jax.ShapeDtypeStruct((n, D_MODEL), F32),
        grid=(n // tm, D_FF // tk),
        in_specs=[
            pl.BlockSpec((tm, D_MODEL), lambda i, k: (i, 0)),
            pl.BlockSpec((1, D_MODEL), lambda i, k: (0, 0)),
            pl.BlockSpec((D_MODEL, tk), lambda i, k: (0, k)),
            pl.BlockSpec((tk, D_MODEL), lambda i, k: (k, 0)),
        ],
        out_specs=pl.BlockSpec((tm, D_MODEL), lambda i, k: (i, 0)),
        scratch_shapes=[pltpu.VMEM((tm, D_MODEL), BF16), pltpu.VMEM((tm, D_MODEL), F32)],
        compiler_params=_cparams(("parallel", "arbitrary")),
        name="mlp",
    )(x, g, wu, wd)


def _rope_tables(pos):
    half = HEAD_DIM // 2
    inv = ROPE_THETA ** (-jnp.arange(half, dtype=F32) / half)
    ang = pos.astype(F32)[:, None] * inv[None, :]
    cos, sin = jnp.cos(ang), jnp.sin(ang)
    cos128 = jnp.tile(cos, (1, LANES // half))
    sin128 = jnp.tile(jnp.concatenate([-sin, sin], axis=-1), (1, LANES // HEAD_DIM))
    return cos128, sin128


def _pack_weights(p):
    dt0 = sum(_SECTION_WIDTHS[:5])
    w_front = p['w_in'].astype(BF16)
    w_gate = w_front[:, dt0 + N_SSM_HEADS:]
    w_dt = jnp.pad(p['w_in'][:, dt0:dt0 + N_SSM_HEADS],
                   ((0, 0), (0, LANES - N_SSM_HEADS))).astype(BF16)
    return dict(
        w_front=w_front, w_gate=w_gate, w_dt=w_dt,
        wa=p['w_branch_a'].astype(BF16), wb=p['w_branch_b'].astype(BF16),
        wo=p['w_out'].astype(BF16), wu=p['w_up'].astype(BF16), wd=p['w_down'].astype(BF16),
    )


def _mixer_common(x2d, pos_tables, p, w, tm, tn, seq_len, act_dtype):
    idx = np.arange(LANES)
    bsel = jnp.asarray((idx[:, None] // HEAD_DIM == idx[None, :] // HEAD_DIM).astype(np.float32),
                       dtype=BF16)
    q_scale = (HEAD_DIM ** -0.5) * LOG2E
    qg = (jnp.tile(p['q_norm_g'], LANES // HEAD_DIM) * q_scale)[None, :]
    kg = jnp.tile(p['k_norm_g'], LANES // HEAD_DIM)[None, :]
    cos_t, sin_t = pos_tables
    return _in_proj(x2d, p['norm_mix_g'][None, :], w['w_front'], w['w_gate'], w['w_dt'], qg, kg,
                    cos_t, sin_t,
                    bsel, tm, tn, seq_len, act_dtype)


def _finish(o, y_ssm, ga, gb, x2d, p, w, tm):
    x1 = _merge(o, y_ssm, ga, gb, x2d, w['wa'], w['wb'], w['wo'], tm)
    return _mlp(x1, p['norm_mlp_g'][None, :], w['wu'], w['wd'], tm, 1024)


def _lam(p, lam_init):
    lam = (jnp.exp(jnp.sum(p['lambda_q1'].astype(F32) * p['lambda_k1'].astype(F32)))
           - jnp.exp(jnp.sum(p['lambda_q2'].astype(F32) * p['lambda_k2'].astype(F32))) + lam_init)
    return lam.reshape(1, 1).astype(F32)


def _prompt_layer(x, p, w, lam, lam_init):
    b, t, _ = x.shape
    n = b * t
    tm = min(512, t)
    x2d = x.reshape(n, D_MODEL)
    tables = _rope_tables(jnp.arange(t))
    assert t >= SUBLANES
    q, ktf, vf, z, xbc, xtail, dt, ga, gb = _mixer_common(x2d, tables, p, w, min(1024, t), 512, t,
                                                          BF16)
    blk = min(512, t)
    o = _prompt_attention(q.reshape(b, t, -1), ktf, vf.reshape(b, t, -1), lam,
                          p['subln_g'][None, :], 1.0 - lam_init, blk)
    conv0 = jnp.zeros((b, CONV_W - 1, CONV_DIM), F32)
    y_ssm, new_ssm = _ssd(xbc.reshape(b, t, CONV_DIM), z.reshape(b, t, D_INNER),
                          dt.reshape(b, t, LANES), conv0, None, p, None, SSD_CHUNK)
    y = _finish(o.reshape(n, -1), y_ssm.reshape(n, -1), ga, gb, x2d, p, w, tm)
    new_conv = xtail.reshape(b, -1, SUBLANES, CONV_DIM)[:, -1, SUBLANES - (CONV_W - 1):]
    k_out = jnp.transpose(ktf.reshape(b, N_HEADS, 2, HEAD_DIM, t), (0, 4, 1, 2, 3))
    return (y.reshape(b, t, D_MODEL), k_out, vf.reshape(b, t, N_HEADS, V_DIM), new_conv, new_ssm)


def _sample_layer(x, p, w, lam, lam_init, cache_k, cache_v, page_table, layer, conv_state,
                  ssm_state):
    b, t, _ = x.shape
    assert t <= DEC_ROWS
    n = b * t
    tm = n
    past = page_table.shape[1] * PAGE_SIZE
    x2d = x.reshape(n, D_MODEL)
    cos_t, sin_t = _rope_tables(past + jnp.arange(t))
    tables = (jnp.tile(cos_t, (b, 1)), jnp.tile(sin_t, (b, 1)))
    q, kf, vf, z, xbc, _, dt, ga, gb = _mixer_common(x2d, tables, p, w, tm, 1024, None, F32)

    q5 = q.reshape(b, t, N_HEADS, 2, HEAD_DIM)
    q5 = jnp.pad(q5, ((0, 0), (0, DEC_ROWS - t), (0, 0), (0, 0), (0, 0)))
    eye_c = jnp.eye(2, dtype=BF16)
    wq = jnp.einsum('bthcd,cC->bhctCd', q5, eye_c).reshape(b, N_HEADS, QROWS, V_DIM).astype(F32)
    pad_t = lambda a: jnp.pad(a.reshape(b, t, -1), ((0, 0), (0, DEC_ROWS - t), (0, 0)))
    ckt = jnp.transpose(cache_k, (0, 1, 3, 4, 5, 2)).reshape(
        cache_k.shape[0], cache_k.shape[1], N_HEADS, V_DIM, PAGE_SIZE)
    cv = cache_v.reshape(cache_v.shape[0], cache_v.shape[1], PAGE_SIZE * N_HEADS, V_DIM)
    o = _sample_attention(page_table, lam, wq, pad_t(kf), pad_t(vf), p['subln_g'][None, :], ckt,
                          cv, layer, t, 1.0 - lam_init)
    o = o[:, :t].reshape(n, ATT_WIDTH).astype(BF16)

    tpad = DEC_ROWS
    pad_c = lambda a: jnp.pad(a.reshape(b, t, -1), ((0, 0), (0, tpad - t), (0, 0)))
    xbc3 = xbc.reshape(b, t, CONV_DIM)
    y_ssm, new_ssm = _ssd(pad_c(xbc), pad_c(z), pad_c(dt), conv_state, ssm_state, p, t, tpad)
    y_ssm = y_ssm[:, :t].reshape(n, D_INNER)
    y = _finish(o, y_ssm, ga, gb, x2d, p, w, tm)
    new_conv = jnp.concatenate([conv_state, xbc3], axis=1)[:, -(CONV_W - 1):]
    return (y.reshape(b, t, D_MODEL), kf.reshape(b, t, N_HEADS, 2, HEAD_DIM),
            vf.reshape(b, t, N_HEADS, V_DIM), new_conv, new_ssm)


def kernel(x_prompt, x_sample, cache_k, cache_v, state_conv, state_ssm, page_table, norm_mix_g, w_in, q_norm_g, k_norm_g, lambda_q1, lambda_k1, lambda_q2, lambda_k2, subln_g, conv_w, conv_b, dt_bias, a_log, d_skip, ssm_norm_g, w_branch_a, w_branch_b, w_out, norm_mlp_g, w_up, w_down):
    depth = w_in.shape[0]
    yp, ys = x_prompt, x_sample
    outs = [[] for _ in range(8)]
    for l in range(depth):
        p = {
            'norm_mix_g': norm_mix_g[l], 'w_in': w_in[l], 'q_norm_g': q_norm_g[l], 'k_norm_g': k_norm_g[l],
            'lambda_q1': lambda_q1[l], 'lambda_k1': lambda_k1[l], 'lambda_q2': lambda_q2[l], 'lambda_k2': lambda_k2[l],
            'subln_g': subln_g[l], 'conv_w': conv_w[l], 'conv_b': conv_b[l], 'dt_bias': dt_bias[l],
            'a_log': a_log[l], 'd_skip': d_skip[l], 'ssm_norm_g': ssm_norm_g[l],
            'w_branch_a': w_branch_a[l], 'w_branch_b': w_branch_b[l], 'w_out': w_out[l],
            'norm_mlp_g': norm_mlp_g[l], 'w_up': w_up[l], 'w_down': w_down[l],
        }
        lam_init = 0.8 - 0.6 * math.exp(-0.3 * l)
        lam = _lam(p, lam_init)
        w = _pack_weights(p)
        yp, kp, vp, cp, sp = _prompt_layer(yp, p, w, lam, lam_init)
        ys, ks, vs, cs, ss = _sample_layer(ys, p, w, lam, lam_init, cache_k, cache_v, page_table, l,
                                           state_conv[l], state_ssm[l])
        for lst, val in zip(outs, (kp, vp, cp, sp, ks, vs, cs, ss)):
            lst.append(val)
    return (yp, ys) + tuple(jnp.stack(o) for o in outs)
```

```python
import functools
import math

import numpy as np
import jax
import jax.numpy as jnp
from jax import lax
from jax.experimental import pallas as pl
from jax.experimental.pallas import tpu as pltpu

F32 = jnp.float32
BF16 = jnp.bfloat16

D_MODEL = 1024
N_HEADS = 8
HEAD_DIM = 64
V_DIM = 2 * HEAD_DIM
ATT_WIDTH = N_HEADS * V_DIM
ROPE_THETA = 10000.0
D_INNER = 2 * D_MODEL
SSM_HEAD_DIM = 64
N_SSM_HEADS = D_INNER // SSM_HEAD_DIM
N_GROUPS = 4
HEADS_PER_GROUP = N_SSM_HEADS // N_GROUPS
GROUP_WIDTH = D_INNER // N_GROUPS
D_STATE = 128
CONV_W = 4
CONV_DIM = D_INNER + 2 * N_GROUPS * D_STATE
SSD_CHUNK = 128
D_FF = 4 * D_MODEL
PAGE_SIZE = 128
EPS = 1e-6
LOG2E = 1.4426950408889634

LANES = 128
SUBLANES = 8
VMEM_LIMIT_BYTES = 56 * 1024 * 1024

NEG_BIG = -1e30

_SECTION_WIDTHS = (ATT_WIDTH, ATT_WIDTH, ATT_WIDTH, D_INNER, CONV_DIM, D_MODEL, D_MODEL)
MAIN_PROJ_TILE = 1024

PAGES_PER_STEP = 8
DEC_ROWS = 8
QROWS = 2 * DEC_ROWS


def _cparams(sem):
    return pltpu.CompilerParams(dimension_semantics=sem, vmem_limit_bytes=VMEM_LIMIT_BYTES)


def _dot(a, b):
    return jnp.dot(a, b, preferred_element_type=F32)


def _dot_nt(a, b):
    return lax.dot_general(a, b, (((1,), (1,)), ((), ())), preferred_element_type=F32)


def _exp2(x):
    return jnp.exp2(x)


def _rope_norm(acc, cos, sin, bsel):
    lane = lax.broadcasted_iota(jnp.int32, (1, LANES), 1)
    first_half = (lane % HEAD_DIM) < (HEAD_DIM // 2)
    outs = []
    for hb in range(acc.shape[1] // LANES):
        xh = acc[:, hb * LANES:(hb + 1) * LANES]
        ms = _dot((xh * xh).astype(BF16), bsel)
        y = xh * lax.rsqrt(ms + EPS)
        partner = jnp.where(first_half,
                            pltpu.roll(y, LANES - HEAD_DIM // 2, 1),
                            pltpu.roll(y, HEAD_DIM // 2, 1))
        outs.append(y * cos + partner * sin)
    return jnp.concatenate(outs, axis=-1)


def _qk_proj_kernel(x_ref, g_ref, w_ref, cosq_ref, sinq_ref, cosk_ref, sink_ref, bsel_ref,
                    q_ref, k_ref, h_ref, *, n_q, transpose_k):
    j = pl.program_id(1)

    @pl.when(j == 0)
    def _():
        x = x_ref[...]
        ms = jnp.mean(x * x, axis=-1, keepdims=True)
        h_ref[...] = (x * lax.rsqrt(ms + EPS) * g_ref[...]).astype(BF16)

    acc = _dot(h_ref[...], w_ref[...])

    @pl.when(j < n_q)
    def _():
        r = _rope_norm(acc, cosq_ref[...], sinq_ref[...], bsel_ref[...])
        q_ref[...] = r.astype(BF16)

    @pl.when(j >= n_q)
    def _():
        r = _rope_norm(acc, cosk_ref[...], sink_ref[...], bsel_ref[...])
        if transpose_k:
            k_ref[0] = r.T
        else:
            k_ref[...] = r


def _main_proj_kernel(h_ref, wf_ref, wg_ref, wdt_ref,
                      v_ref, z_ref, xbc_ref, xtail_ref, dt_ref, ga_ref, gb_ref, *, starts):
    J_V, J_Z, J_X, J_GA, J_GB, _ = starts
    j = pl.program_id(1)

    @pl.when(j == 0)
    def _():
        dt_ref[...] = _dot(h_ref[...], wdt_ref[...])

    @pl.when(j < J_GA)
    def _():
        acc = _dot(h_ref[...], wf_ref[...])

        @pl.when(j < J_Z)
        def _():
            v_ref[...] = acc

        @pl.when((j >= J_Z) & (j < J_X))
        def _():
            z_ref[...] = acc.astype(z_ref.dtype)

        @pl.when(j >= J_X)
        def _():
            xbc_ref[...] = acc.astype(xbc_ref.dtype)
            xtail_ref[0] = acc[acc.shape[0] - SUBLANES:, :]

    @pl.when(j >= J_GA)
    def _():
        acc = _dot(h_ref[...], wg_ref[...])

        @pl.when(j < J_GB)
        def _():
            ga_ref[...] = acc.astype(ga_ref.dtype)

        @pl.when(j >= J_GB)
        def _():
            gb_ref[...] = acc.astype(gb_ref.dtype)


def _block_maps(n_rows):
    def row_after(stop):
        if n_rows == 1:
            return lambda i, j: i
        return lambda i, j: jnp.where(j >= stop, jnp.minimum(i + 1, n_rows - 1), i)

    def section(start, stop):
        last = stop - start - 1
        row_of = row_after(stop)

        def index(i, j):
            col = jnp.clip(j - start, 0, last)
            if n_rows == 1:
                return i, col
            moved_on = (j >= stop) & (i + 1 < n_rows)
            return row_of(i, j), jnp.where(moved_on, 0, col)
        return index

    return row_after, section


def _in_proj(x, g, w_front, w_gate, w_dt, rope_tables, bsel, tm, tn_qk, seq_len, act_dtype):
    n = x.shape[0]
    n_rows = n // tm
    rope_blocks = rope_tables[0].shape[0] // tm
    row_after, section = _block_maps(n_rows)
    const = lambda i, j: (0, 0)

    n_q = ATT_WIDTH // tn_qk
    n_qk = 2 * n_q
    transpose_k = seq_len is not None
    tiles_per_seq = seq_len // tm if transpose_k else 1
    if transpose_k:
        k_shape = (n // seq_len, ATT_WIDTH, seq_len)

        def k_map(i, j):
            r, c = section(n_q, n_qk)(i, j)
            return r // tiles_per_seq, c, r % tiles_per_seq
        k_spec = pl.BlockSpec((1, tn_qk, tm), k_map)
    else:
        k_shape = (n, ATT_WIDTH)
        k_spec = pl.BlockSpec((tm, tn_qk), section(n_q, n_qk))
    x_map = lambda i, j: (row_after(1)(i, j), 0)
    q, k, h = pl.pallas_call(
        functools.partial(_qk_proj_kernel, n_q=n_q, transpose_k=transpose_k),
        out_shape=[jax.ShapeDtypeStruct((n, ATT_WIDTH), BF16),
                   jax.ShapeDtypeStruct(k_shape, F32),
                   jax.ShapeDtypeStruct((n, D_MODEL), BF16)],
        grid=(n_rows, n_qk),
        in_specs=[
            pl.BlockSpec((tm, D_MODEL), x_map),
            pl.BlockSpec((1, D_MODEL), const),
            pl.BlockSpec((D_MODEL, tn_qk), lambda i, j: (0, j)),
            pl.BlockSpec((tm, LANES), lambda i, j: (i % rope_blocks, 0)),
            pl.BlockSpec((tm, LANES), lambda i, j: (i % rope_blocks, 0)),
            pl.BlockSpec((tm, LANES), lambda i, j: (i % rope_blocks, 0)),
            pl.BlockSpec((tm, LANES), lambda i, j: (i % rope_blocks, 0)),
            pl.BlockSpec((LANES, LANES), const),
        ],
        out_specs=[pl.BlockSpec((tm, tn_qk), section(0, n_q)),
                   k_spec,
                   pl.BlockSpec((tm, D_MODEL), lambda i, j: (i, 0))],
        compiler_params=_cparams(("arbitrary", "arbitrary")),
        name="qk_proj",
    )(x, g, w_front, *rope_tables, bsel)

    tn = MAIN_PROJ_TILE
    starts = tuple(int(c) // tn for c in np.cumsum((0,) + _SECTION_WIDTHS[2:]))
    J_V, J_Z, J_X, J_GA, J_GB, n_tiles = starts
    qk_tiles = (2 * ATT_WIDTH) // tn
    v, z, xbc, xtail, dt, ga, gb = pl.pallas_call(
        functools.partial(_main_proj_kernel, starts=starts),
        out_shape=[
            jax.ShapeDtypeStruct((n, ATT_WIDTH), F32),
            jax.ShapeDtypeStruct((n, D_INNER), act_dtype),
            jax.ShapeDtypeStruct((n, CONV_DIM), act_dtype),
            jax.ShapeDtypeStruct((n_rows, SUBLANES, CONV_DIM), F32),
            jax.ShapeDtypeStruct((n, LANES), F32),
            jax.ShapeDtypeStruct((n, D_MODEL), act_dtype),
            jax.ShapeDtypeStruct((n, D_MODEL), act_dtype),
        ],
        grid=(n_rows, n_tiles),
        in_specs=[
            pl.BlockSpec((tm, D_MODEL), lambda i, j: (i, 0)),
            pl.BlockSpec((D_MODEL, tn), lambda i, j: (0, qk_tiles + jnp.minimum(j, J_GA - 1))),
            pl.BlockSpec((D_MODEL, tn), lambda i, j: (0, jnp.maximum(j - J_GA, 0))),
            pl.BlockSpec((D_MODEL, LANES), const),
        ],
        out_specs=[
            pl.BlockSpec((tm, tn), section(J_V, J_Z)),
            pl.BlockSpec((tm, tn), section(J_Z, J_X)),
            pl.BlockSpec((tm, tn), section(J_X, J_GA)),
            pl.BlockSpec((1, SUBLANES, tn), lambda i, j: (section(J_X, J_GA)(i, j)[0], 0,
                                                          section(J_X, J_GA)(i, j)[1])),
            pl.BlockSpec((tm, LANES), lambda i, j: (row_after(1)(i, j), 0)),
            pl.BlockSpec((tm, tn), section(J_GA, J_GB)),
            pl.BlockSpec((tm, tn), section(J_GB, n_tiles)),
        ],
        compiler_params=_cparams(("arbitrary", "arbitrary")),
        name="main_proj",
    )(h, w_front, w_gate, w_dt)
    return q, k, v, z, xbc, xtail, dt, ga, gb


def _subln(o, g, out_scale):
    ms = jnp.mean(o * o, axis=-1, keepdims=True)
    return o * lax.rsqrt(ms + EPS) * g * out_scale


def _prompt_attn_kernel(lam_ref, q_ref, kt_ref, v_ref, g_ref, o_ref, *, blk, out_scale):
    t = q_ref.shape[1]
    lam = lam_ref[0, 0]
    row = lax.broadcasted_iota(jnp.int32, (blk, blk), 0)
    col = lax.broadcasted_iota(jnp.int32, (blk, blk), 1)
    causal = col <= row
    kt = kt_ref[0].astype(BF16)
    v = v_ref[0].astype(BF16)
    for qi in range(t // blk):
        r0 = qi * blk
        q = q_ref[0, r0:r0 + blk, :]
        ps = []
        for c in range(2):
            cs = slice(c * HEAD_DIM, (c + 1) * HEAD_DIM)
            qc = q[:, cs]
            s_d = jnp.where(causal, _dot(qc, kt[cs, r0:r0 + blk]), NEG_BIG)
            m = jnp.max(s_d, axis=-1, keepdims=True)
            p_o = None
            if qi > 0:
                s_o = _dot(qc, kt[cs, 0:r0])
                m = jnp.maximum(m, jnp.max(s_o, axis=-1, keepdims=True))
                p_o = _exp2(s_o - m)
            p_d = _exp2(s_d - m)
            l = jnp.sum(p_d, axis=-1, keepdims=True)
            if qi > 0:
                l = l + jnp.sum(p_o, axis=-1, keepdims=True)
            ps.append((p_o, p_d, l))
        w0 = 1.0 / ps[0][2]
        w1 = lam / ps[1][2]
        o = _dot((ps[0][1] * w0 - ps[1][1] * w1).astype(BF16), v[r0:r0 + blk, :])
        if qi > 0:
            o = o + _dot((ps[0][0] * w0 - ps[1][0] * w1).astype(BF16), v[0:r0, :])
        o_ref[0, r0:r0 + blk, :] = _subln(o, g_ref[...], out_scale).astype(o_ref.dtype)


def _prompt_attention(q, kt, v, lam, subln_g, out_scale, blk):
    b, t, _ = q.shape
    kern = functools.partial(_prompt_attn_kernel, blk=blk, out_scale=out_scale)
    return pl.pallas_call(
        kern,
        out_shape=jax.ShapeDtypeStruct((b, t, ATT_WIDTH), BF16),
        grid=(b, N_HEADS),
        in_specs=[
            pl.BlockSpec(memory_space=pltpu.SMEM),
            pl.BlockSpec((1, t, V_DIM), lambda bi, h: (bi, 0, h)),
            pl.BlockSpec((1, V_DIM, t), lambda bi, h: (bi, h, 0)),
            pl.BlockSpec((1, t, V_DIM), lambda bi, h: (bi, 0, h)),
            pl.BlockSpec((1, V_DIM), lambda bi, h: (0, 0)),
        ],
        out_specs=pl.BlockSpec((1, t, V_DIM), lambda bi, h: (bi, 0, h)),
        compiler_params=_cparams(("parallel", "parallel")),
        name="prompt_attention",
    )(lam, q, kt, v, subln_g)


def _sample_attn_kernel(pt_ref, lam_ref, wq_ref, knew_ref, vnew_ref, g_ref, *rest,
                        n_valid, out_scale):
    kp_refs = rest[:PAGES_PER_STEP]
    vp_refs = rest[PAGES_PER_STEP:2 * PAGES_PER_STEP]
    o_ref = rest[2 * PAGES_PER_STEP]
    m_ref, l_ref, acc_ref = rest[2 * PAGES_PER_STEP + 1:]
    j = pl.program_id(1)

    @pl.when(j == 0)
    def _():
        m_ref[...] = jnp.full(m_ref.shape, NEG_BIG, F32)
        l_ref[...] = jnp.zeros(l_ref.shape, F32)
        acc_ref[...] = jnp.zeros(acc_ref.shape, F32)

    s = jnp.concatenate(
        [jnp.concatenate([_dot(wq_ref[0, h], kp_refs[p][0, 0, h]) for p in range(PAGES_PER_STEP)],
                         axis=-1) for h in range(N_HEADS)], axis=0)
    m = m_ref[...]
    m_new = jnp.maximum(m, jnp.max(s, axis=-1, keepdims=True))
    alpha = _exp2(m - m_new)
    pr = _exp2(s - m_new)
    l_ref[...] = alpha * l_ref[...] + jnp.sum(pr, axis=-1, keepdims=True)
    m_ref[...] = m_new
    pvs = []
    for h in range(N_HEADS):
        ph = pr[h * QROWS:(h + 1) * QROWS]
        v_rows = pl.ds(h, PAGE_SIZE, stride=N_HEADS)
        pv = _dot(ph[:, :PAGE_SIZE], vp_refs[0][0, 0, v_rows, :])
        for p in range(1, PAGES_PER_STEP):
            pv = pv + _dot(ph[:, p * PAGE_SIZE:(p + 1) * PAGE_SIZE], vp_refs[p][0, 0, v_rows, :])
        pvs.append(pv)
    acc_ref[...] = alpha * acc_ref[...] + jnp.concatenate(pvs, axis=0)

    @pl.when(j == pl.num_programs(1) - 1)
    def _():
        lam = lam_ref[0, 0]
        row_t = lax.broadcasted_iota(jnp.int32, (QROWS, DEC_ROWS), 0) % DEC_ROWS
        col_t = lax.broadcasted_iota(jnp.int32, (QROWS, DEC_ROWS), 1)
        valid = (col_t <= row_t) & (col_t < n_valid)
        outs = []
        for h in range(N_HEADS):
            rs = slice(h * QROWS, (h + 1) * QROWS)
            hs = slice(h * V_DIM, (h + 1) * V_DIM)
            sn = jnp.where(valid, _dot_nt(wq_ref[0, h], knew_ref[0][:, hs]), NEG_BIG)
            mm = m_ref[rs, :]
            mf = jnp.maximum(mm, jnp.max(sn, axis=-1, keepdims=True))
            al = _exp2(mm - mf)
            pn = _exp2(sn - mf)
            lf = al * l_ref[rs, :] + jnp.sum(pn, axis=-1, keepdims=True)
            af = al * acc_ref[rs, :] + _dot(pn, vnew_ref[0][:, hs])
            of = af / lf
            od = of[:DEC_ROWS] - lam * of[DEC_ROWS:]
            outs.append(_subln(od, g_ref[...], out_scale))
        o_ref[0] = jnp.concatenate(outs, axis=-1)


def _sample_attention(page_table, lam, wq, k_new, v_new, subln_g, cache_kt, cache_v, layer,
                      n_valid, out_scale):
    db, n_pages = page_table.shape
    n_steps = n_pages // PAGES_PER_STEP

    def page_map(p, ndim):
        return lambda b, j, pt: (layer, pt[b * n_pages + j * PAGES_PER_STEP + p]) + (0,) * (ndim - 2)

    kern = functools.partial(_sample_attn_kernel, n_valid=n_valid, out_scale=out_scale)
    grid_spec = pltpu.PrefetchScalarGridSpec(
        num_scalar_prefetch=1,
        grid=(db, n_steps),
        in_specs=[
            pl.BlockSpec(memory_space=pltpu.SMEM),
            pl.BlockSpec((1, N_HEADS, QROWS, V_DIM), lambda b, j, pt: (b, 0, 0, 0)),
            pl.BlockSpec((1, DEC_ROWS, ATT_WIDTH), lambda b, j, pt: (b, 0, 0)),
            pl.BlockSpec((1, DEC_ROWS, ATT_WIDTH), lambda b, j, pt: (b, 0, 0)),
            pl.BlockSpec((1, V_DIM), lambda b, j, pt: (0, 0)),
        ] + [pl.BlockSpec((1, 1, N_HEADS, V_DIM, PAGE_SIZE), page_map(p, 5))
             for p in range(PAGES_PER_STEP)]
          + [pl.BlockSpec((1, 1, PAGE_SIZE * N_HEADS, V_DIM), page_map(p, 4))
             for p in range(PAGES_PER_STEP)],
        out_specs=pl.BlockSpec((1, DEC_ROWS, ATT_WIDTH), lambda b, j, pt: (b, 0, 0)),
        scratch_shapes=[
            pltpu.VMEM((N_HEADS * QROWS, 1), F32),
            pltpu.VMEM((N_HEADS * QROWS, 1), F32),
            pltpu.VMEM((N_HEADS * QROWS, V_DIM), F32),
        ],
    )
    return pl.pallas_call(
        kern,
        out_shape=jax.ShapeDtypeStruct((db, DEC_ROWS, ATT_WIDTH), F32),
        grid_spec=grid_spec,
        compiler_params=_cparams(("parallel", "arbitrary")),
        name="sample_attention",
    )(page_table.reshape(-1), lam, wq, k_new, v_new, subln_g,
      *([cache_kt] * PAGES_PER_STEP), *([cache_v] * PAGES_PER_STEP))


def _split3(v):
    hi = v.astype(BF16)
    r1 = v - hi.astype(F32)
    mid = r1.astype(BF16)
    lo = (r1 - mid.astype(F32)).astype(BF16)
    return hi, mid, lo


def _softplus(x):
    return jnp.maximum(x, 0.0) + jnp.log1p(jnp.exp(-jnp.abs(x)))


def _ssd_kernel(*refs, chunk, n_valid, has_init):
    (xbc_ref, z_ref, dt_ref, dtt_ref, cinit_ref) = refs[:5]
    k = 5
    if has_init:
        sinit_ref = refs[k]
        k += 1
    (cw_ref, cb_ref, bias_ref, biast_ref, alog_ref, alogt_ref, dskip_ref, ng_ref,
     ltri_ref, utri_ref, e64_ref, eseg_ref) = refs[k:k + 12]
    y_ref, sout_ref = refs[k + 12:k + 14]
    tail_ref, st_ref = refs[k + 14:]
    c = pl.program_id(1)
    L = chunk

    @pl.when(c == 0)
    def _():
        tail_ref[...] = cinit_ref[0]
        if has_init:
            st_ref[...] = sinit_ref[0].T
        else:
            st_ref[...] = jnp.zeros(st_ref.shape, F32)

    xraw = xbc_ref[0].astype(F32)
    head =jnp.concatenate([tail_ref[...], xraw[:SUBLANES]], axis=0)
    conv = cb_ref[...]
    conv_head = cb_ref[...]
    for kk in range(CONV_W):
        back = CONV_W - 1 - kk
        wk = cw_ref[kk:kk + 1, :]
        shifted = pltpu.roll(xraw, back, 0) if back else xraw
        conv = conv + shifted * wk
        conv_head = conv_head + head[SUBLANES - back:2 * SUBLANES - back] * wk
    conv = jnp.concatenate([conv_head, conv[SUBLANES:]], axis=0) if L > SUBLANES else conv_head
    tail_ref[...] = xraw[L - SUBLANES:, :]
    xact = conv * jax.nn.sigmoid(conv)
    xs = xact[:, :D_INNER]

    nh = N_SSM_HEADS
    dt = _softplus(dt_ref[0][:, :nh] + bias_ref[...])
    dtt = _softplus(dtt_ref[0] + biast_ref[...])
    if n_valid is not None:
        tok = c * L + lax.broadcasted_iota(jnp.int32, (L, 1), 0)
        dt = jnp.where(tok < n_valid, dt, 0.0)
        tokt = c * L + lax.broadcasted_iota(jnp.int32, (1, L), 1)
        dtt = jnp.where(tokt < n_valid, dtt, 0.0)
    da = dt * (-jnp.exp(alog_ref[...]) * LOG2E)
    dat = dtt * (-jnp.exp(alogt_ref[...]) * LOG2E)
    ltri = ltri_ref[...]
    utri = utri_ref[...]
    a = sum(_dot(ltri, part) for part in _split3(da))
    at = sum(_dot(part, utri) for part in _split3(dat))
    a_last = a[L - 1:L, :]
    w = _exp2(a_last - a) * dt

    def expand(v, e_ref):
        return _dot(jnp.concatenate(_split3(v), axis=-1), e_ref[...])

    aseg = expand(a, eseg_ref)
    a64 = expand(a, e64_ref)
    w64 = expand(w, e64_ref)
    ea64 = _exp2(a64)
    xw = (xs * w64).astype(BF16)

    row = lax.broadcasted_iota(jnp.int32, (L, L), 0)
    col = lax.broadcasted_iota(jnp.int32, (L, L), 1)
    causal = col <= row
    lane = lax.broadcasted_iota(jnp.int32, (1, LANES), 1)
    low = lane < SSM_HEAD_DIM

    ys = []
    for g in range(N_GROUPS):
        bg = xact[:, D_INNER + g * D_STATE:D_INNER + (g + 1) * D_STATE]
        cg = xact[:, D_INNER + (N_GROUPS + g) * D_STATE:D_INNER + (N_GROUPS + g + 1) * D_STATE]
        bgb = bg.astype(BF16)
        cgb = cg.astype(BF16)
        gmat = _dot_nt(cgb, bgb)
        gs = slice(g * GROUP_WIDTH, (g + 1) * GROUP_WIDTH)
        st_g = st_ref[:, gs]
        y_off = _dot(cgb, st_g.astype(BF16)) * ea64[:, gs]
        y_diag = []
        for pr in range(HEADS_PER_GROUP // 2):
            ms = []
            for hh in range(2):
                h = g * HEADS_PER_GROUP + 2 * pr + hh
                seg = aseg[:, h * L:(h + 1) * L] - at[h:h + 1, :]
                lm = jnp.where(causal, _exp2(seg), 0.0)
                ms.append((gmat * lm * dtt[h:h + 1, :]).astype(BF16))
            h0 = g * HEADS_PER_GROUP + 2 * pr
            xpair = xs[:, h0 * SSM_HEAD_DIM:(h0 + 2) * SSM_HEAD_DIM]
            xstack = jnp.concatenate([jnp.where(low, xpair, 0.0), jnp.where(low, 0.0, xpair)],
                                     axis=0).astype(BF16)
            y_diag.append(_dot(jnp.concatenate(ms, axis=-1), xstack))
        ys.append(jnp.concatenate(y_diag, axis=-1) + y_off)
        cd = ea64[L - 1:L, gs]
        st_ref[:, gs] = cd * st_g + _dot(bg.T.astype(BF16), xw[:, gs])
    y = jnp.concatenate(ys, axis=-1) + dskip_ref[...] * xs
    zz = z_ref[0].astype(F32)
    y = y * (zz * jax.nn.sigmoid(zz))
    outs = []
    for g in range(N_GROUPS):
        yg = y[:, g * GROUP_WIDTH:(g + 1) * GROUP_WIDTH]
        ms = jnp.mean(yg * yg, axis=-1, keepdims=True)
        outs.append(yg * lax.rsqrt(ms + EPS) * ng_ref[:, g * GROUP_WIDTH:(g + 1) * GROUP_WIDTH])
    y_ref[0] = jnp.concatenate(outs, axis=-1).astype(y_ref.dtype)

    @pl.when(c == pl.num_programs(1) - 1)
    def _():
        sout_ref[0] = st_ref[...].T


def _ssd_constants(chunk):
    idx = np.arange(chunk)
    ltri = (idx[None, :] <= idx[:, None]).astype(np.float32)
    utri = ltri.T
    k3 = np.arange(3 * N_SSM_HEADS) % N_SSM_HEADS
    e64 = (k3[:, None] == (np.arange(D_INNER)[None, :] // SSM_HEAD_DIM)).astype(np.float32)
    eseg = (k3[:, None] == (np.arange(N_SSM_HEADS * chunk)[None, :] // chunk)).astype(np.float32)
    return tuple(jnp.asarray(m, dtype=BF16) for m in (ltri, utri, e64, eseg))


def _ssd(xbc, z, dt_raw, conv_init, ssm_init, p, n_valid, chunk):
    b, t, _ = xbc.shape
    nc = t // chunk
    has_init = ssm_init is not None
    dtt = jnp.swapaxes(dt_raw[:, :, :N_SSM_HEADS], 1, 2)
    cinit = jnp.zeros((b, SUBLANES, CONV_DIM), F32)
    cinit = cinit.at[:, SUBLANES - (CONV_W - 1):, :].set(conv_init)
    consts = _ssd_constants(chunk)
    bmap = lambda bi, c: (bi, c, 0)
    const2 = lambda bi, c: (0, 0)
    args = [xbc, z, dt_raw, dtt, cinit]
    in_specs = [
        pl.BlockSpec((1, chunk, CONV_DIM), bmap),
        pl.BlockSpec((1, chunk, D_INNER), bmap),
        pl.BlockSpec((1, chunk, LANES), bmap),
        pl.BlockSpec((1, N_SSM_HEADS, chunk), lambda bi, c: (bi, 0, c)),
        pl.BlockSpec((1, SUBLANES, CONV_DIM), lambda bi, c: (bi, 0, 0)),
    ]
    if has_init:
        args.append(ssm_init.reshape(b, D_INNER, D_STATE))
        in_specs.append(pl.BlockSpec((1, D_INNER, D_STATE), lambda bi, c: (bi, 0, 0)))
    params = [p['conv_w'], p['conv_b'][None, :], p['dt_bias'][None, :], p['dt_bias'][:, None],
              p['a_log'][None, :], p['a_log'][:, None],
              jnp.repeat(p['d_skip'], SSM_HEAD_DIM)[None, :], p['ssm_norm_g'][None, :]]
    args += params + list(consts)
    in_specs += [pl.BlockSpec(a.shape, const2) for a in params + list(consts)]
    kern = functools.partial(_ssd_kernel, chunk=chunk, n_valid=n_valid, has_init=has_init)
    y, s_out = pl.pallas_call(
        kern,
        out_shape=[jax.ShapeDtypeStruct((b, t, D_INNER), BF16),
                   jax.ShapeDtypeStruct((b, D_INNER, D_STATE), F32)],
        grid=(b, nc),
        in_specs=in_specs,
        out_specs=[pl.BlockSpec((1, chunk, D_INNER), bmap),
                   pl.BlockSpec((1, D_INNER, D_STATE), lambda bi, c: (bi, 0, 0))],
        scratch_shapes=[pltpu.VMEM((SUBLANES, CONV_DIM), F32),
                        pltpu.VMEM((D_STATE, D_INNER), F32)],
        compiler_params=_cparams(("parallel", "arbitrary")),
        name="ssd",
    )(*args)
    return y, s_out.reshape(b, N_SSM_HEADS, SSM_HEAD_DIM, D_STATE)


def _merge_kernel(o_ref, y_ref, ga_ref, gb_ref, x_ref, wa_ref, wb_ref, wo_ref, out_ref):
    ba = _dot(o_ref[...], wa_ref[...])
    bb = _dot(y_ref[...], wb_ref[...])
    mix = (jax.nn.sigmoid(ga_ref[...].astype(F32)) * ba
           + jax.nn.sigmoid(gb_ref[...].astype(F32)) * bb)
    out_ref[...] = x_ref[...] + _dot(mix.astype(BF16), wo_ref[...])


def _merge(o, y, ga, gb, x, wa, wb, wo, tm):
    n = x.shape[0]
    row = lambda i: (i, 0)
    const = lambda i: (0, 0)
    return pl.pallas_call(
        _merge_kernel,
        out_shape=jax.ShapeDtypeStruct((n, D_MODEL), F32),
        grid=(n // tm,),
        in_specs=[
            pl.BlockSpec((tm, ATT_WIDTH), row),
            pl.BlockSpec((tm, D_INNER), row),
            pl.BlockSpec((tm, D_MODEL), row),
            pl.BlockSpec((tm, D_MODEL), row),
            pl.BlockSpec((tm, D_MODEL), row),
            pl.BlockSpec((ATT_WIDTH, D_MODEL), const),
            pl.BlockSpec((D_INNER, D_MODEL), const),
            pl.BlockSpec((D_MODEL, D_MODEL), const),
        ],
        out_specs=pl.BlockSpec((tm, D_MODEL), row),
        compiler_params=_cparams(("parallel",)),
        name="merge",
    )(o, y, ga, gb, x, wa, wb, wo)


def _mlp_kernel(x_ref, g_ref, wu_ref, wd_ref, out_ref, h_ref, acc_ref):
    k = pl.program_id(1)

    @pl.when(k == 0)
    def _():
        x = x_ref[...]
        ms = jnp.mean(x * x, axis=-1, keepdims=True)
        h_ref[...] = (x * lax.rsqrt(ms + EPS) * g_ref[...]).astype(BF16)
        acc_ref[...] = x

    u = _dot(h_ref[...], wu_ref[...])
    r = jnp.square(jnp.maximum(u, 0.0))
    acc_ref[...] += _dot(r.astype(BF16), wd_ref[...])

    @pl.when(k == pl.num_programs(1) - 1)
    def _():
        out_ref[...] = acc_ref[...]


def _mlp(x, g, wu, wd, tm, tk):
    n = x.shape[0]
    return pl.pallas_call(
        _mlp_kernel,
        out_shape=jax.ShapeDtypeStruct((n, D_MODEL), F32),
        grid=(n // tm, D_FF // tk),
        in_specs=[
            pl.BlockSpec((tm, D_MODEL), lambda i, k: (i, 0)),
            pl.BlockSpec((1, D_MODEL), lambda i, k: (0, 0)),
            pl.BlockSpec((D_MODEL, tk), lambda i, k: (0, k)),
            pl.BlockSpec((tk, D_MODEL), lambda i, k: (k, 0)),
        ],
        out_specs=pl.BlockSpec((tm, D_MODEL), lambda i, k: (i, 0)),
        scratch_shapes=[pltpu.VMEM((tm, D_MODEL), BF16), pltpu.VMEM((tm, D_MODEL), F32)],
        compiler_params=_cparams(("parallel", "arbitrary")),
        name="mlp",
    )(x, g, wu, wd)


def _rope_tables(pos):
    half = HEAD_DIM // 2
    inv = ROPE_THETA ** (-jnp.arange(half, dtype=F32) / half)
    ang = pos.astype(F32)[:, None] * inv[None, :]
    cos, sin = jnp.cos(ang), jnp.sin(ang)
    cos128 = jnp.tile(cos, (1, LANES // half))
    sin128 = jnp.tile(jnp.concatenate([-sin, sin], axis=-1), (1, LANES // HEAD_DIM))
    return cos128, sin128


def _pack_weights(p):
    dt0 = sum(_SECTION_WIDTHS[:5])
    w_front = p['w_in'].astype(BF16)
    w_gate = w_front[:, dt0 + N_SSM_HEADS:]
    w_dt = jnp.pad(p['w_in'][:, dt0:dt0 + N_SSM_HEADS],
                   ((0, 0), (0, LANES - N_SSM_HEADS))).astype(BF16)
    return dict(
        w_front=w_front, w_gate=w_gate, w_dt=w_dt,
        wa=p['w_branch_a'].astype(BF16), wb=p['w_branch_b'].astype(BF16),
        wo=p['w_out'].astype(BF16), wu=p['w_up'].astype(BF16), wd=p['w_down'].astype(BF16),
    )


def _mixer_common(x2d, pos_tables, p, w, tm, tn, seq_len, act_dtype):
    idx = np.arange(LANES)
    bsel = jnp.asarray((idx[:, None] // HEAD_DIM == idx[None, :] // HEAD_DIM).astype(np.float32)
                       / HEAD_DIM, dtype=BF16)
    q_scale = (HEAD_DIM ** -0.5) * LOG2E
    cos_t, sin_t = pos_tables
    tables = []
    for gain in (jnp.tile(p['q_norm_g'], LANES // HEAD_DIM) * q_scale,
                 jnp.tile(p['k_norm_g'], LANES // HEAD_DIM)):
        tables += [cos_t * gain[None, :], sin_t * jnp.roll(gain, HEAD_DIM // 2)[None, :]]
    return _in_proj(x2d, p['norm_mix_g'][None, :], w['w_front'], w['w_gate'], w['w_dt'], tables,
                    bsel, tm, tn, seq_len, act_dtype)


def _finish(o, y_ssm, ga, gb, x2d, p, w, tm):
    x1 = _merge(o, y_ssm, ga, gb, x2d, w['wa'], w['wb'], w['wo'], tm)
    return _mlp(x1, p['norm_mlp_g'][None, :], w['wu'], w['wd'], tm, 2048)


def _lam(p, lam_init):
    lam = (jnp.exp(jnp.sum(p['lambda_q1'].astype(F32) * p['lambda_k1'].astype(F32)))
           - jnp.exp(jnp.sum(p['lambda_q2'].astype(F32) * p['lambda_k2'].astype(F32))) + lam_init)
    return lam.reshape(1, 1).astype(F32)


def _prompt_layer(x, p, w, lam, lam_init):
    b, t, _ = x.shape
    n = b * t
    tm = min(512, t)
    x2d = x.reshape(n, D_MODEL)
    tables = _rope_tables(jnp.arange(t))
    assert t >= SUBLANES
    q, ktf, vf, z, xbc, xtail, dt, ga, gb = _mixer_common(x2d, tables, p, w, min(1024, t), 512, t,
                                                          BF16)
    blk = min(512, t)
    o = _prompt_attention(q.reshape(b, t, -1), ktf, vf.reshape(b, t, -1), lam,
                          p['subln_g'][None, :], 1.0 - lam_init, blk)
    conv0 = jnp.zeros((b, CONV_W - 1, CONV_DIM), F32)
    y_ssm, new_ssm = _ssd(xbc.reshape(b, t, CONV_DIM), z.reshape(b, t, D_INNER),
                          dt.reshape(b, t, LANES), conv0, None, p, None, SSD_CHUNK)
    y = _finish(o.reshape(n, -1), y_ssm.reshape(n, -1), ga, gb, x2d, p, w, tm)
    new_conv = xtail.reshape(b, -1, SUBLANES, CONV_DIM)[:, -1, SUBLANES - (CONV_W - 1):]
    k_out = jnp.transpose(ktf.reshape(b, N_HEADS, 2, HEAD_DIM, t), (0, 4, 1, 2, 3))
    return (y.reshape(b, t, D_MODEL), k_out, vf.reshape(b, t, N_HEADS, V_DIM), new_conv, new_ssm)


def _sample_layer(x, p, w, lam, lam_init, cache_k, cache_v, page_table, layer, conv_state,
                  ssm_state):
    b, t, _ = x.shape
    assert t <= DEC_ROWS
    n = b * t
    tm = n
    past = page_table.shape[1] * PAGE_SIZE
    x2d = x.reshape(n, D_MODEL)
    cos_t, sin_t = _rope_tables(past + jnp.arange(t))
    tables = (jnp.tile(cos_t, (b, 1)), jnp.tile(sin_t, (b, 1)))
    q, kf, vf, z, xbc, _, dt, ga, gb = _mixer_common(x2d, tables, p, w, tm, 1024, None, F32)

    q5 = q.reshape(b, t, N_HEADS, 2, HEAD_DIM)
    q5 = jnp.pad(q5, ((0, 0), (0, DEC_ROWS - t), (0, 0), (0, 0), (0, 0)))
    eye_c = jnp.eye(2, dtype=BF16)
    wq = jnp.einsum('bthcd,cC->bhctCd', q5, eye_c).reshape(b, N_HEADS, QROWS, V_DIM).astype(F32)
    pad_t = lambda a: jnp.pad(a.reshape(b, t, -1), ((0, 0), (0, DEC_ROWS - t), (0, 0)))
    ckt = jnp.transpose(cache_k, (0, 1, 3, 4, 5, 2)).reshape(
        cache_k.shape[0], cache_k.shape[1], N_HEADS, V_DIM, PAGE_SIZE)
    cv = cache_v.reshape(cache_v.shape[0], cache_v.shape[1], PAGE_SIZE * N_HEADS, V_DIM)
    o = _sample_attention(page_table, lam, wq, pad_t(kf), pad_t(vf), p['subln_g'][None, :], ckt,
                          cv, layer, t, 1.0 - lam_init)
    o = o[:, :t].reshape(n, ATT_WIDTH).astype(BF16)

    tpad = DEC_ROWS
    pad_c = lambda a: jnp.pad(a.reshape(b, t, -1), ((0, 0), (0, tpad - t), (0, 0)))
    xbc3 = xbc.reshape(b, t, CONV_DIM)
    y_ssm, new_ssm = _ssd(pad_c(xbc), pad_c(z), pad_c(dt), conv_state, ssm_state, p, t, tpad)
    y_ssm = y_ssm[:, :t].reshape(n, D_INNER)
    y = _finish(o, y_ssm, ga, gb, x2d, p, w, tm)
    new_conv = jnp.concatenate([conv_state, xbc3], axis=1)[:, -(CONV_W - 1):]
    return (y.reshape(b, t, D_MODEL), kf.reshape(b, t, N_HEADS, 2, HEAD_DIM),
            vf.reshape(b, t, N_HEADS, V_DIM), new_conv, new_ssm)


def kernel(x_prompt, x_sample, cache_k, cache_v, state_conv, state_ssm, page_table, norm_mix_g, w_in, q_norm_g, k_norm_g, lambda_q1, lambda_k1, lambda_q2, lambda_k2, subln_g, conv_w, conv_b, dt_bias, a_log, d_skip, ssm_norm_g, w_branch_a, w_branch_b, w_out, norm_mlp_g, w_up, w_down):
    depth = w_in.shape[0]
    yp, ys = x_prompt, x_sample
    outs = [[] for _ in range(8)]
    for l in range(depth):
        p = {
            'norm_mix_g': norm_mix_g[l], 'w_in': w_in[l], 'q_norm_g': q_norm_g[l], 'k_norm_g': k_norm_g[l],
            'lambda_q1': lambda_q1[l], 'lambda_k1': lambda_k1[l], 'lambda_q2': lambda_q2[l], 'lambda_k2': lambda_k2[l],
            'subln_g': subln_g[l], 'conv_w': conv_w[l], 'conv_b': conv_b[l], 'dt_bias': dt_bias[l],
            'a_log': a_log[l], 'd_skip': d_skip[l], 'ssm_norm_g': ssm_norm_g[l],
            'w_branch_a': w_branch_a[l], 'w_branch_b': w_branch_b[l], 'w_out': w_out[l],
            'norm_mlp_g': norm_mlp_g[l], 'w_up': w_up[l], 'w_down': w_down[l],
        }
        lam_init = 0.8 - 0.6 * math.exp(-0.3 * l)
        lam = _lam(p, lam_init)
        w = _pack_weights(p)
        yp, kp, vp, cp, sp = _prompt_layer(yp, p, w, lam, lam_init)
        ys, ks, vs, cs, ss = _sample_layer(ys, p, w, lam, lam_init, cache_k, cache_v, page_table, l,
                                           state_conv[l], state_ssm[l])
        for lst, val in zip(outs, (kp, vp, cp, sp, ks, vs, cs, ss)):
            lst.append(val)
    return (yp, ys) + tuple(jnp.stack(o) for o in outs)
```

```python
import functools
import math

import numpy as np
import jax
import jax.numpy as jnp
from jax import lax
from jax.experimental import pallas as pl
from jax.experimental.pallas import tpu as pltpu

F32 = jnp.float32
BF16 = jnp.bfloat16

D_MODEL = 1024
N_HEADS = 8
HEAD_DIM = 64
V_DIM = 2 * HEAD_DIM
ATT_WIDTH = N_HEADS * V_DIM
ROPE_THETA = 10000.0
D_INNER = 2 * D_MODEL
SSM_HEAD_DIM = 64
N_SSM_HEADS = D_INNER // SSM_HEAD_DIM
N_GROUPS = 4
HEADS_PER_GROUP = N_SSM_HEADS // N_GROUPS
GROUP_WIDTH = D_INNER // N_GROUPS
D_STATE = 128
CONV_W = 4
CONV_DIM = D_INNER + 2 * N_GROUPS * D_STATE
SSD_CHUNK = 128
D_FF = 4 * D_MODEL
PAGE_SIZE = 128
EPS = 1e-6
LOG2E = 1.4426950408889634

LANES = 128
SUBLANES = 8
VMEM_LIMIT_BYTES = 56 * 1024 * 1024

NEG_BIG = -1e30

_SECTION_WIDTHS = (ATT_WIDTH, ATT_WIDTH, ATT_WIDTH, D_INNER, CONV_DIM, D_MODEL, D_MODEL)
MAIN_PROJ_TILE = 1024

PAGES_PER_STEP = 8
DEC_ROWS = 8
QROWS = 2 * DEC_ROWS


def _cparams(sem):
    return pltpu.CompilerParams(dimension_semantics=sem, vmem_limit_bytes=VMEM_LIMIT_BYTES)


def _dot(a, b):
    return jnp.dot(a, b, preferred_element_type=F32)


def _dot_nt(a, b):
    return lax.dot_general(a, b, (((1,), (1,)), ((), ())), preferred_element_type=F32)


def _exp2(x):
    return jnp.exp2(x)


def _rope_norm(acc, cos, sin, bsel):
    lane = lax.broadcasted_iota(jnp.int32, (1, LANES), 1)
    first_half = (lane % HEAD_DIM) < (HEAD_DIM // 2)
    outs = []
    for hb in range(acc.shape[1] // LANES):
        xh = acc[:, hb * LANES:(hb + 1) * LANES]
        ms = _dot((xh * xh).astype(BF16), bsel)
        y = xh * lax.rsqrt(ms + EPS)
        partner = jnp.where(first_half,
                            pltpu.roll(y, LANES - HEAD_DIM // 2, 1),
                            pltpu.roll(y, HEAD_DIM // 2, 1))
        outs.append(y * cos + partner * sin)
    return jnp.concatenate(outs, axis=-1)


def _qk_proj_kernel(x_ref, g_ref, w_ref, cosq_ref, sinq_ref, cosk_ref, sink_ref, bsel_ref,
                    q_ref, k_ref, h_ref, *, n_q, transpose_k):
    j = pl.program_id(1)

    @pl.when(j == 0)
    def _():
        x = x_ref[...]
        ms = jnp.mean(x * x, axis=-1, keepdims=True)
        h_ref[...] = (x * lax.rsqrt(ms + EPS) * g_ref[...]).astype(BF16)

    acc = _dot(h_ref[...], w_ref[...])

    @pl.when(j < n_q)
    def _():
        r = _rope_norm(acc, cosq_ref[...], sinq_ref[...], bsel_ref[...])
        q_ref[...] = r.astype(BF16)

    @pl.when(j >= n_q)
    def _():
        r = _rope_norm(acc, cosk_ref[...], sink_ref[...], bsel_ref[...])
        if transpose_k:
            k_ref[0] = r.T
        else:
            k_ref[...] = r


def _main_proj_kernel(h_ref, wf_ref, wg_ref, wdt_ref,
                      v_ref, z_ref, xbc_ref, xtail_ref, dt_ref, ga_ref, gb_ref, *, starts):
    J_V, J_Z, J_X, J_GA, J_GB, _ = starts
    j = pl.program_id(1)

    @pl.when(j == 0)
    def _():
        dt_ref[...] = _dot(h_ref[...], wdt_ref[...])

    @pl.when(j < J_GA)
    def _():
        acc = _dot(h_ref[...], wf_ref[...])

        @pl.when(j < J_Z)
        def _():
            v_ref[...] = acc

        @pl.when((j >= J_Z) & (j < J_X))
        def _():
            z_ref[...] = acc.astype(z_ref.dtype)

        @pl.when(j >= J_X)
        def _():
            xbc_ref[...] = acc.astype(xbc_ref.dtype)
            xtail_ref[0] = acc[acc.shape[0] - SUBLANES:, :]

    @pl.when(j >= J_GA)
    def _():
        acc = _dot(h_ref[...], wg_ref[...])

        @pl.when(j < J_GB)
        def _():
            ga_ref[...] = acc.astype(ga_ref.dtype)

        @pl.when(j >= J_GB)
        def _():
            gb_ref[...] = acc.astype(gb_ref.dtype)


def _block_maps(n_rows):
    def row_after(stop):
        if n_rows == 1:
            return lambda i, j: i
        return lambda i, j: jnp.where(j >= stop, jnp.minimum(i + 1, n_rows - 1), i)

    def section(start, stop):
        last = stop - start - 1
        row_of = row_after(stop)

        def index(i, j):
            col = jnp.clip(j - start, 0, last)
            if n_rows == 1:
                return i, col
            moved_on = (j >= stop) & (i + 1 < n_rows)
            return row_of(i, j), jnp.where(moved_on, 0, col)
        return index

    return row_after, section


def _in_proj(x, g, w_front, w_gate, w_dt, rope_tables, bsel, tm, tn_qk, seq_len, act_dtype):
    n = x.shape[0]
    n_rows = n // tm
    rope_blocks = rope_tables[0].shape[0] // tm
    row_after, section = _block_maps(n_rows)
    const = lambda i, j: (0, 0)

    n_q = ATT_WIDTH // tn_qk
    n_qk = 2 * n_q
    transpose_k = seq_len is not None
    tiles_per_seq = seq_len // tm if transpose_k else 1
    if transpose_k:
        k_shape = (n // seq_len, ATT_WIDTH, seq_len)

        def k_map(i, j):
            r, c = section(n_q, n_qk)(i, j)
            return r // tiles_per_seq, c, r % tiles_per_seq
        k_spec = pl.BlockSpec((1, tn_qk, tm), k_map)
    else:
        k_shape = (n, ATT_WIDTH)
        k_spec = pl.BlockSpec((tm, tn_qk), section(n_q, n_qk))
    x_map = lambda i, j: (row_after(1)(i, j), 0)
    q, k, h = pl.pallas_call(
        functools.partial(_qk_proj_kernel, n_q=n_q, transpose_k=transpose_k),
        out_shape=[jax.ShapeDtypeStruct((n, ATT_WIDTH), BF16),
                   jax.ShapeDtypeStruct(k_shape, F32),
                   jax.ShapeDtypeStruct((n, D_MODEL), BF16)],
        grid=(n_rows, n_qk),
        in_specs=[
            pl.BlockSpec((tm, D_MODEL), x_map),
            pl.BlockSpec((1, D_MODEL), const),
            pl.BlockSpec((D_MODEL, tn_qk), lambda i, j: (0, j)),
            pl.BlockSpec((tm, LANES), lambda i, j: (i % rope_blocks, 0)),
            pl.BlockSpec((tm, LANES), lambda i, j: (i % rope_blocks, 0)),
            pl.BlockSpec((tm, LANES), lambda i, j: (i % rope_blocks, 0)),
            pl.BlockSpec((tm, LANES), lambda i, j: (i % rope_blocks, 0)),
            pl.BlockSpec((LANES, LANES), const),
        ],
        out_specs=[pl.BlockSpec((tm, tn_qk), section(0, n_q)),
                   k_spec,
                   pl.BlockSpec((tm, D_MODEL), lambda i, j: (i, 0))],
        compiler_params=_cparams(("arbitrary", "arbitrary")),
        name="qk_proj",
    )(x, g, w_front, *rope_tables, bsel)

    tn = MAIN_PROJ_TILE
    starts = tuple(int(c) // tn for c in np.cumsum((0,) + _SECTION_WIDTHS[2:]))
    J_V, J_Z, J_X, J_GA, J_GB, n_tiles = starts
    qk_tiles = (2 * ATT_WIDTH) // tn
    v, z, xbc, xtail, dt, ga, gb = pl.pallas_call(
        functools.partial(_main_proj_kernel, starts=starts),
        out_shape=[
            jax.ShapeDtypeStruct((n, ATT_WIDTH), F32),
            jax.ShapeDtypeStruct((n, D_INNER), act_dtype),
            jax.ShapeDtypeStruct((n, CONV_DIM), act_dtype),
            jax.ShapeDtypeStruct((n_rows, SUBLANES, CONV_DIM), F32),
            jax.ShapeDtypeStruct((n, LANES), F32),
            jax.ShapeDtypeStruct((n, D_MODEL), act_dtype),
            jax.ShapeDtypeStruct((n, D_MODEL), act_dtype),
        ],
        grid=(n_rows, n_tiles),
        in_specs=[
            pl.BlockSpec((tm, D_MODEL), lambda i, j: (i, 0)),
            pl.BlockSpec((D_MODEL, tn), lambda i, j: (0, qk_tiles + jnp.minimum(j, J_GA - 1))),
            pl.BlockSpec((D_MODEL, tn), lambda i, j: (0, jnp.maximum(j - J_GA, 0))),
            pl.BlockSpec((D_MODEL, LANES), const),
        ],
        out_specs=[
            pl.BlockSpec((tm, tn), section(J_V, J_Z)),
            pl.BlockSpec((tm, tn), section(J_Z, J_X)),
            pl.BlockSpec((tm, tn), section(J_X, J_GA)),
            pl.BlockSpec((1, SUBLANES, tn), lambda i, j: (section(J_X, J_GA)(i, j)[0], 0,
                                                          section(J_X, J_GA)(i, j)[1])),
            pl.BlockSpec((tm, LANES), lambda i, j: (row_after(1)(i, j), 0)),
            pl.BlockSpec((tm, tn), section(J_GA, J_GB)),
            pl.BlockSpec((tm, tn), section(J_GB, n_tiles)),
        ],
        compiler_params=_cparams(("arbitrary", "arbitrary")),
        name="main_proj",
    )(h, w_front, w_gate, w_dt)
    return q, k, v, z, xbc, xtail, dt, ga, gb


def _subln(o, g, out_scale):
    ms = jnp.mean(o * o, axis=-1, keepdims=True)
    return o * lax.rsqrt(ms + EPS) * g * out_scale


def _prompt_attn_kernel(lam_ref, q_ref, kt_ref, v_ref, g_ref, o_ref, *, blk, out_scale):
    t = q_ref.shape[1]
    lam = lam_ref[0, 0]
    row = lax.broadcasted_iota(jnp.int32, (blk, blk), 0)
    col = lax.broadcasted_iota(jnp.int32, (blk, blk), 1)
    causal = col <= row
    kt = kt_ref[0].astype(BF16)
    v = v_ref[0].astype(BF16)
    for qi in range(t // blk):
        r0 = qi * blk
        q = q_ref[0, r0:r0 + blk, :]
        ps = []
        for c in range(2):
            cs = slice(c * HEAD_DIM, (c + 1) * HEAD_DIM)
            qc = q[:, cs]
            s_d = jnp.where(causal, _dot(qc, kt[cs, r0:r0 + blk]), NEG_BIG)
            m = jnp.max(s_d, axis=-1, keepdims=True)
            p_o = None
            if qi > 0:
                s_o = _dot(qc, kt[cs, 0:r0])
                m = jnp.maximum(m, jnp.max(s_o, axis=-1, keepdims=True))
                p_o = _exp2(s_o - m)
            p_d = _exp2(s_d - m)
            l = jnp.sum(p_d, axis=-1, keepdims=True)
            if qi > 0:
                l = l + jnp.sum(p_o, axis=-1, keepdims=True)
            ps.append((p_o, p_d, l))
        w0 = 1.0 / ps[0][2]
        w1 = lam / ps[1][2]
        o = _dot((ps[0][1] * w0 - ps[1][1] * w1).astype(BF16), v[r0:r0 + blk, :])
        if qi > 0:
            o = o + _dot((ps[0][0] * w0 - ps[1][0] * w1).astype(BF16), v[0:r0, :])
        o_ref[0, r0:r0 + blk, :] = _subln(o, g_ref[...], out_scale).astype(o_ref.dtype)


def _prompt_attention(q, kt, v, lam, subln_g, out_scale, blk):
    b, t, _ = q.shape
    kern = functools.partial(_prompt_attn_kernel, blk=blk, out_scale=out_scale)
    return pl.pallas_call(
        kern,
        out_shape=jax.ShapeDtypeStruct((b, t, ATT_WIDTH), BF16),
        grid=(b, N_HEADS),
        in_specs=[
            pl.BlockSpec(memory_space=pltpu.SMEM),
            pl.BlockSpec((1, t, V_DIM), lambda bi, h: (bi, 0, h)),
            pl.BlockSpec((1, V_DIM, t), lambda bi, h: (bi, h, 0)),
            pl.BlockSpec((1, t, V_DIM), lambda bi, h: (bi, 0, h)),
            pl.BlockSpec((1, V_DIM), lambda bi, h: (0, 0)),
        ],
        out_specs=pl.BlockSpec((1, t, V_DIM), lambda bi, h: (bi, 0, h)),
        compiler_params=_cparams(("parallel", "parallel")),
        name="prompt_attention",
    )(lam, q, kt, v, subln_g)


def _sample_attn_kernel(pt_ref, lam_ref, wq_ref, knew_ref, vnew_ref, g_ref, *rest,
                        n_valid, out_scale):
    kp_refs = rest[:PAGES_PER_STEP]
    vp_refs = rest[PAGES_PER_STEP:2 * PAGES_PER_STEP]
    o_ref = rest[2 * PAGES_PER_STEP]
    m_ref, l_ref, acc_ref = rest[2 * PAGES_PER_STEP + 1:]
    j = pl.program_id(1)

    @pl.when(j == 0)
    def _():
        m_ref[...] = jnp.full(m_ref.shape, NEG_BIG, F32)
        l_ref[...] = jnp.zeros(l_ref.shape, F32)
        acc_ref[...] = jnp.zeros(acc_ref.shape, F32)

    s = jnp.concatenate(
        [jnp.concatenate([_dot(wq_ref[0, h], kp_refs[p][0, 0, h]) for p in range(PAGES_PER_STEP)],
                         axis=-1) for h in range(N_HEADS)], axis=0)
    m = m_ref[...]
    m_new = jnp.maximum(m, jnp.max(s, axis=-1, keepdims=True))
    alpha = _exp2(m - m_new)
    pr = _exp2(s - m_new)
    l_ref[...] = alpha * l_ref[...] + jnp.sum(pr, axis=-1, keepdims=True)
    m_ref[...] = m_new
    pvs = []
    for h in range(N_HEADS):
        ph = pr[h * QROWS:(h + 1) * QROWS]
        v_rows = pl.ds(h, PAGE_SIZE, stride=N_HEADS)
        pv = _dot(ph[:, :PAGE_SIZE], vp_refs[0][0, 0, v_rows, :])
        for p in range(1, PAGES_PER_STEP):
            pv = pv + _dot(ph[:, p * PAGE_SIZE:(p + 1) * PAGE_SIZE], vp_refs[p][0, 0, v_rows, :])
        pvs.append(pv)
    acc_ref[...] = alpha * acc_ref[...] + jnp.concatenate(pvs, axis=0)

    @pl.when(j == pl.num_programs(1) - 1)
    def _():
        lam = lam_ref[0, 0]
        row_t = lax.broadcasted_iota(jnp.int32, (QROWS, DEC_ROWS), 0) % DEC_ROWS
        col_t = lax.broadcasted_iota(jnp.int32, (QROWS, DEC_ROWS), 1)
        valid = (col_t <= row_t) & (col_t < n_valid)
        outs = []
        for h in range(N_HEADS):
            rs = slice(h * QROWS, (h + 1) * QROWS)
            hs = slice(h * V_DIM, (h + 1) * V_DIM)
            sn = jnp.where(valid, _dot_nt(wq_ref[0, h], knew_ref[0][:, hs]), NEG_BIG)
            mm = m_ref[rs, :]
            mf = jnp.maximum(mm, jnp.max(sn, axis=-1, keepdims=True))
            al = _exp2(mm - mf)
            pn = _exp2(sn - mf)
            lf = al * l_ref[rs, :] + jnp.sum(pn, axis=-1, keepdims=True)
            af = al * acc_ref[rs, :] + _dot(pn, vnew_ref[0][:, hs])
            of = af / lf
            od = of[:DEC_ROWS] - lam * of[DEC_ROWS:]
            outs.append(_subln(od, g_ref[...], out_scale))
        o_ref[0] = jnp.concatenate(outs, axis=-1)


def _sample_attention(page_table, lam, wq, k_new, v_new, subln_g, cache_kt, cache_v, layer,
                      n_valid, out_scale):
    db, n_pages = page_table.shape
    n_steps = n_pages // PAGES_PER_STEP

    def page_map(p, ndim):
        return lambda b, j, pt: (layer, pt[b * n_pages + j * PAGES_PER_STEP + p]) + (0,) * (ndim - 2)

    kern = functools.partial(_sample_attn_kernel, n_valid=n_valid, out_scale=out_scale)
    grid_spec = pltpu.PrefetchScalarGridSpec(
        num_scalar_prefetch=1,
        grid=(db, n_steps),
        in_specs=[
            pl.BlockSpec(memory_space=pltpu.SMEM),
            pl.BlockSpec((1, N_HEADS, QROWS, V_DIM), lambda b, j, pt: (b, 0, 0, 0)),
            pl.BlockSpec((1, DEC_ROWS, ATT_WIDTH), lambda b, j, pt: (b, 0, 0)),
            pl.BlockSpec((1, DEC_ROWS, ATT_WIDTH), lambda b, j, pt: (b, 0, 0)),
            pl.BlockSpec((1, V_DIM), lambda b, j, pt: (0, 0)),
        ] + [pl.BlockSpec((1, 1, N_HEADS, V_DIM, PAGE_SIZE), page_map(p, 5))
             for p in range(PAGES_PER_STEP)]
          + [pl.BlockSpec((1, 1, PAGE_SIZE * N_HEADS, V_DIM), page_map(p, 4))
             for p in range(PAGES_PER_STEP)],
        out_specs=pl.BlockSpec((1, DEC_ROWS, ATT_WIDTH), lambda b, j, pt: (b, 0, 0)),
        scratch_shapes=[
            pltpu.VMEM((N_HEADS * QROWS, 1), F32),
            pltpu.VMEM((N_HEADS * QROWS, 1), F32),
            pltpu.VMEM((N_HEADS * QROWS, V_DIM), F32),
        ],
    )
    return pl.pallas_call(
        kern,
        out_shape=jax.ShapeDtypeStruct((db, DEC_ROWS, ATT_WIDTH), F32),
        grid_spec=grid_spec,
        compiler_params=_cparams(("parallel", "arbitrary")),
        name="sample_attention",
    )(page_table.reshape(-1), lam, wq, k_new, v_new, subln_g,
      *([cache_kt] * PAGES_PER_STEP), *([cache_v] * PAGES_PER_STEP))


def _split3(v):
    hi = v.astype(BF16)
    r1 = v - hi.astype(F32)
    mid = r1.astype(BF16)
    lo = (r1 - mid.astype(F32)).astype(BF16)
    return hi, mid, lo


def _softplus(x):
    return jnp.maximum(x, 0.0) + jnp.log1p(jnp.exp(-jnp.abs(x)))


def _ssd_kernel(*refs, chunk, n_valid, has_init):
    (xbc_ref, z_ref, dt_ref, dtt_ref, cinit_ref) = refs[:5]
    k = 5
    if has_init:
        sinit_ref = refs[k]
        k += 1
    (cw_ref, cb_ref, bias_ref, biast_ref, alog_ref, alogt_ref, dskip_ref, ng_ref,
     ltri_ref, utri_ref, e64_ref, eseg_ref) = refs[k:k + 12]
    y_ref, sout_ref = refs[k + 12:k + 14]
    tail_ref, st_ref = refs[k + 14:]
    c = pl.program_id(1)
    L = chunk

    @pl.when(c == 0)
    def _():
        tail_ref[...] = cinit_ref[0]
        if has_init:
            st_ref[...] = sinit_ref[0].T
        else:
            st_ref[...] = jnp.zeros(st_ref.shape, F32)

    xraw = xbc_ref[0].astype(F32)
    head =jnp.concatenate([tail_ref[...], xraw[:SUBLANES]], axis=0)
    conv = cb_ref[...]
    conv_head = cb_ref[...]
    for kk in range(CONV_W):
        back = CONV_W - 1 - kk
        wk = cw_ref[kk:kk + 1, :]
        shifted = pltpu.roll(xraw, back, 0) if back else xraw
        conv = conv + shifted * wk
        conv_head = conv_head + head[SUBLANES - back:2 * SUBLANES - back] * wk
    conv = jnp.concatenate([conv_head, conv[SUBLANES:]], axis=0) if L > SUBLANES else conv_head
    tail_ref[...] = xraw[L - SUBLANES:, :]
    xact = conv * jax.nn.sigmoid(conv)
    xs = xact[:, :D_INNER]

    nh = N_SSM_HEADS
    dt = _softplus(dt_ref[0][:, :nh] + bias_ref[...])
    dtt = _softplus(dtt_ref[0] + biast_ref[...])
    if n_valid is not None:
        tok = c * L + lax.broadcasted_iota(jnp.int32, (L, 1), 0)
        dt = jnp.where(tok < n_valid, dt, 0.0)
        tokt = c * L + lax.broadcasted_iota(jnp.int32, (1, L), 1)
        dtt = jnp.where(tokt < n_valid, dtt, 0.0)
    da = dt * (-jnp.exp(alog_ref[...]) * LOG2E)
    dat = dtt * (-jnp.exp(alogt_ref[...]) * LOG2E)
    ltri = ltri_ref[...]
    utri = utri_ref[...]
    a = sum(_dot(ltri, part) for part in _split3(da))
    at = sum(_dot(part, utri) for part in _split3(dat))
    a_last = a[L - 1:L, :]
    w = _exp2(a_last - a) * dt

    def expand(v, e_ref):
        return _dot(jnp.concatenate(_split3(v), axis=-1), e_ref[...])

    aseg = expand(a, eseg_ref)
    a64 = expand(a, e64_ref)
    w64 = expand(w, e64_ref)
    ea64 = _exp2(a64)
    xw = (xs * w64).astype(BF16)

    row = lax.broadcasted_iota(jnp.int32, (L, L), 0)
    col = lax.broadcasted_iota(jnp.int32, (L, L), 1)
    causal = col <= row
    lane = lax.broadcasted_iota(jnp.int32, (1, LANES), 1)
    low = lane < SSM_HEAD_DIM

    ys = []
    for g in range(N_GROUPS):
        bg = xact[:, D_INNER + g * D_STATE:D_INNER + (g + 1) * D_STATE]
        cg = xact[:, D_INNER + (N_GROUPS + g) * D_STATE:D_INNER + (N_GROUPS + g + 1) * D_STATE]
        bgb = bg.astype(BF16)
        cgb = cg.astype(BF16)
        gmat = _dot_nt(cgb, bgb)
        gs = slice(g * GROUP_WIDTH, (g + 1) * GROUP_WIDTH)
        st_g = st_ref[:, gs]
        y_off = _dot(cgb, st_g.astype(BF16)) * ea64[:, gs]
        y_diag = []
        for pr in range(HEADS_PER_GROUP // 2):
            ms = []
            for hh in range(2):
                h = g * HEADS_PER_GROUP + 2 * pr + hh
                seg = aseg[:, h * L:(h + 1) * L] - at[h:h + 1, :]
                lm = jnp.where(causal, _exp2(seg), 0.0)
                ms.append((gmat * lm * dtt[h:h + 1, :]).astype(BF16))
            h0 = g * HEADS_PER_GROUP + 2 * pr
            xpair = xs[:, h0 * SSM_HEAD_DIM:(h0 + 2) * SSM_HEAD_DIM]
            xstack = jnp.concatenate([jnp.where(low, xpair, 0.0), jnp.where(low, 0.0, xpair)],
                                     axis=0).astype(BF16)
            y_diag.append(_dot(jnp.concatenate(ms, axis=-1), xstack))
        ys.append(jnp.concatenate(y_diag, axis=-1) + y_off)
        cd = ea64[L - 1:L, gs]
        st_ref[:, gs] = cd * st_g + _dot(bg.T.astype(BF16), xw[:, gs])
    y = jnp.concatenate(ys, axis=-1) + dskip_ref[...] * xs
    zz = z_ref[0].astype(F32)
    y = y * (zz * jax.nn.sigmoid(zz))
    outs = []
    for g in range(N_GROUPS):
        yg = y[:, g * GROUP_WIDTH:(g + 1) * GROUP_WIDTH]
        ms = jnp.mean(yg * yg, axis=-1, keepdims=True)
        outs.append(yg * lax.rsqrt(ms + EPS) * ng_ref[:, g * GROUP_WIDTH:(g + 1) * GROUP_WIDTH])
    y_ref[0] = jnp.concatenate(outs, axis=-1).astype(y_ref.dtype)

    @pl.when(c == pl.num_programs(1) - 1)
    def _():
        sout_ref[0] = st_ref[...].T


def _ssd_constants(chunk):
    idx = np.arange(chunk)
    ltri = (idx[None, :] <= idx[:, None]).astype(np.float32)
    utri = ltri.T
    k3 = np.arange(3 * N_SSM_HEADS) % N_SSM_HEADS
    e64 = (k3[:, None] == (np.arange(D_INNER)[None, :] // SSM_HEAD_DIM)).astype(np.float32)
    eseg = (k3[:, None] == (np.arange(N_SSM_HEADS * chunk)[None, :] // chunk)).astype(np.float32)
    return tuple(jnp.asarray(m, dtype=BF16) for m in (ltri, utri, e64, eseg))


def _ssd(xbc, z, dt_raw, conv_init, ssm_init, p, n_valid, chunk):
    b, t, _ = xbc.shape
    nc = t // chunk
    has_init = ssm_init is not None
    dtt = jnp.swapaxes(dt_raw[:, :, :N_SSM_HEADS], 1, 2)
    cinit = jnp.zeros((b, SUBLANES, CONV_DIM), F32)
    cinit = cinit.at[:, SUBLANES - (CONV_W - 1):, :].set(conv_init)
    consts = _ssd_constants(chunk)
    bmap = lambda bi, c: (bi, c, 0)
    const2 = lambda bi, c: (0, 0)
    args = [xbc, z, dt_raw, dtt, cinit]
    in_specs = [
        pl.BlockSpec((1, chunk, CONV_DIM), bmap),
        pl.BlockSpec((1, chunk, D_INNER), bmap),
        pl.BlockSpec((1, chunk, LANES), bmap),
        pl.BlockSpec((1, N_SSM_HEADS, chunk), lambda bi, c: (bi, 0, c)),
        pl.BlockSpec((1, SUBLANES, CONV_DIM), lambda bi, c: (bi, 0, 0)),
    ]
    if has_init:
        args.append(ssm_init.reshape(b, D_INNER, D_STATE))
        in_specs.append(pl.BlockSpec((1, D_INNER, D_STATE), lambda bi, c: (bi, 0, 0)))
    params = [p['conv_w'], p['conv_b'][None, :], p['dt_bias'][None, :], p['dt_bias'][:, None],
              p['a_log'][None, :], p['a_log'][:, None],
              jnp.repeat(p['d_skip'], SSM_HEAD_DIM)[None, :], p['ssm_norm_g'][None, :]]
    args += params + list(consts)
    in_specs += [pl.BlockSpec(a.shape, const2) for a in params + list(consts)]
    kern = functools.partial(_ssd_kernel, chunk=chunk, n_valid=n_valid, has_init=has_init)
    y, s_out = pl.pallas_call(
        kern,
        out_shape=[jax.ShapeDtypeStruct((b, t, D_INNER), BF16),
                   jax.ShapeDtypeStruct((b, D_INNER, D_STATE), F32)],
        grid=(b, nc),
        in_specs=in_specs,
        out_specs=[pl.BlockSpec((1, chunk, D_INNER), bmap),
                   pl.BlockSpec((1, D_INNER, D_STATE), lambda bi, c: (bi, 0, 0))],
        scratch_shapes=[pltpu.VMEM((SUBLANES, CONV_DIM), F32),
                        pltpu.VMEM((D_STATE, D_INNER), F32)],
        compiler_params=_cparams(("parallel", "arbitrary")),
        name="ssd",
    )(*args)
    return y, s_out.reshape(b, N_SSM_HEADS, SSM_HEAD_DIM, D_STATE)


def _merge_kernel(o_ref, y_ref, ga_ref, gb_ref, x_ref, wa_ref, wb_ref, wo_ref, out_ref):
    ba = _dot(o_ref[...], wa_ref[...])
    bb = _dot(y_ref[...], wb_ref[...])
    mix = (jax.nn.sigmoid(ga_ref[...].astype(F32)) * ba
           + jax.nn.sigmoid(gb_ref[...].astype(F32)) * bb)
    out_ref[...] = x_ref[...] + _dot(mix.astype(BF16), wo_ref[...])


def _merge(o, y, ga, gb, x, wa, wb, wo, tm):
    n = x.shape[0]
    row = lambda i: (i, 0)
    const = lambda i: (0, 0)
    return pl.pallas_call(
        _merge_kernel,
        out_shape=jax.ShapeDtypeStruct((n, D_MODEL), F32),
        grid=(n // tm,),
        in_specs=[
            pl.BlockSpec((tm, ATT_WIDTH), row),
            pl.BlockSpec((tm, D_INNER), row),
            pl.BlockSpec((tm, D_MODEL), row),
            pl.BlockSpec((tm, D_MODEL), row),
            pl.BlockSpec((tm, D_MODEL), row),
            pl.BlockSpec((ATT_WIDTH, D_MODEL), const),
            pl.BlockSpec((D_INNER, D_MODEL), const),
            pl.BlockSpec((D_MODEL, D_MODEL), const),
        ],
        out_specs=pl.BlockSpec((tm, D_MODEL), row),
        compiler_params=_cparams(("parallel",)),
        name="merge",
    )(o, y, ga, gb, x, wa, wb, wo)


def _mlp_kernel(x_ref, g_ref, wu_ref, wd_ref, out_ref, h_ref, acc_ref):
    k = pl.program_id(1)

    @pl.when(k == 0)
    def _():
        x = x_ref[...]
        ms = jnp.mean(x * x, axis=-1, keepdims=True)
        h_ref[...] = (x * lax.rsqrt(ms + EPS) * g_ref[...]).astype(BF16)
        acc_ref[...] = x

    u = _dot(h_ref[...], wu_ref[...])
    r = jnp.square(jnp.maximum(u, 0.0))
    acc_ref[...] += _dot(r.astype(BF16), wd_ref[...])

    @pl.when(k == pl.num_programs(1) - 1)
    def _():
        out_ref[...] = acc_ref[...]


def _mlp(x, g, wu, wd, tm, tk):
    n = x.shape[0]
    return pl.pallas_call(
        _mlp_kernel,
        out_shape=jax.ShapeDtypeStruct((n, D_MODEL), F32),
        grid=(n // tm, D_FF // tk),
        in_specs=[
            pl.BlockSpec((tm, D_MODEL), lambda i, k: (i, 0)),
            pl.BlockSpec((1, D_MODEL), lambda i, k: (0, 0)),
            pl.BlockSpec((D_MODEL, tk), lambda i, k: (0, k)),
            pl.BlockSpec((tk, D_MODEL), lambda i, k: (k, 0)),
        ],
        out_specs=pl.BlockSpec((tm, D_MODEL), lambda i, k: (i, 0)),
        scratch_shapes=[pltpu.VMEM((tm, D_MODEL), BF16), pltpu.VMEM((tm, D_MODEL), F32)],
        compiler_params=_cparams(("parallel", "arbitrary")),
        name="mlp",
    )(x, g, wu, wd)


def _rope_tables(pos):
    half = HEAD_DIM // 2
    inv = ROPE_THETA ** (-jnp.arange(half, dtype=F32) / half)
    ang = pos.astype(F32)[:, None] * inv[None, :]
    cos, sin = jnp.cos(ang), jnp.sin(ang)
    cos128 = jnp.tile(cos, (1, LANES // half))
    sin128 = jnp.tile(jnp.concatenate([-sin, sin], axis=-1), (1, LANES // HEAD_DIM))
    return cos128, sin128


def _pack_weights(p):
    dt0 = sum(_SECTION_WIDTHS[:5])
    w_front = p['w_in'].astype(BF16)
    w_gate = w_front[:, dt0 + N_SSM_HEADS:]
    w_dt = jnp.pad(p['w_in'][:, dt0:dt0 + N_SSM_HEADS],
                   ((0, 0), (0, LANES - N_SSM_HEADS))).astype(BF16)
    return dict(
        w_front=w_front, w_gate=w_gate, w_dt=w_dt,
        wa=p['w_branch_a'].astype(BF16), wb=p['w_branch_b'].astype(BF16),
        wo=p['w_out'].astype(BF16), wu=p['w_up'].astype(BF16), wd=p['w_down'].astype(BF16),
    )


def _mixer_common(x2d, pos_tables, p, w, tm, tn, seq_len, act_dtype):
    idx = np.arange(LANES)
    bsel = jnp.asarray((idx[:, None] // HEAD_DIM == idx[None, :] // HEAD_DIM).astype(np.float32)
                       / HEAD_DIM, dtype=BF16)
    q_scale = (HEAD_DIM ** -0.5) * LOG2E
    cos_t, sin_t = pos_tables
    tables = []
    for gain in (jnp.tile(p['q_norm_g'], LANES // HEAD_DIM) * q_scale,
                 jnp.tile(p['k_norm_g'], LANES // HEAD_DIM)):
        tables += [cos_t * gain[None, :], sin_t * jnp.roll(gain, HEAD_DIM // 2)[None, :]]
    return _in_proj(x2d, p['norm_mix_g'][None, :], w['w_front'], w['w_gate'], w['w_dt'], tables,
                    bsel, tm, tn, seq_len, act_dtype)


def _finish(o, y_ssm, ga, gb, x2d, p, w, tm):
    x1 = _merge(o, y_ssm, ga, gb, x2d, w['wa'], w['wb'], w['wo'], tm)
    return _mlp(x1, p['norm_mlp_g'][None, :], w['wu'], w['wd'], min(2 * tm, x1.shape[0]), 2048)


def _lam(p, lam_init):
    lam = (jnp.exp(jnp.sum(p['lambda_q1'].astype(F32) * p['lambda_k1'].astype(F32)))
           - jnp.exp(jnp.sum(p['lambda_q2'].astype(F32) * p['lambda_k2'].astype(F32))) + lam_init)
    return lam.reshape(1, 1).astype(F32)


def _prompt_layer(x, p, w, lam, lam_init):
    b, t, _ = x.shape
    n = b * t
    tm = min(512, t)
    x2d = x.reshape(n, D_MODEL)
    tables = _rope_tables(jnp.arange(t))
    assert t >= SUBLANES
    q, ktf, vf, z, xbc, xtail, dt, ga, gb = _mixer_common(x2d, tables, p, w, min(1024, t), 512, t,
                                                          BF16)
    blk = min(512, t)
    o = _prompt_attention(q.reshape(b, t, -1), ktf, vf.reshape(b, t, -1), lam,
                          p['subln_g'][None, :], 1.0 - lam_init, blk)
    conv0 = jnp.zeros((b, CONV_W - 1, CONV_DIM), F32)
    y_ssm, new_ssm = _ssd(xbc.reshape(b, t, CONV_DIM), z.reshape(b, t, D_INNER),
                          dt.reshape(b, t, LANES), conv0, None, p, None, SSD_CHUNK)
    y = _finish(o.reshape(n, -1), y_ssm.reshape(n, -1), ga, gb, x2d, p, w, tm)
    new_conv = xtail.reshape(b, -1, SUBLANES, CONV_DIM)[:, -1, SUBLANES - (CONV_W - 1):]
    k_out = jnp.transpose(ktf.reshape(b, N_HEADS, 2, HEAD_DIM, t), (0, 4, 1, 2, 3))
    return (y.reshape(b, t, D_MODEL), k_out, vf.reshape(b, t, N_HEADS, V_DIM), new_conv, new_ssm)


def _sample_layer(x, p, w, lam, lam_init, cache_k, cache_v, page_table, layer, conv_state,
                  ssm_state):
    b, t, _ = x.shape
    assert t <= DEC_ROWS
    n = b * t
    tm = n
    past = page_table.shape[1] * PAGE_SIZE
    x2d = x.reshape(n, D_MODEL)
    cos_t, sin_t = _rope_tables(past + jnp.arange(t))
    tables = (jnp.tile(cos_t, (b, 1)), jnp.tile(sin_t, (b, 1)))
    q, kf, vf, z, xbc, _, dt, ga, gb = _mixer_common(x2d, tables, p, w, tm, 1024, None, F32)

    q5 = q.reshape(b, t, N_HEADS, 2, HEAD_DIM)
    q5 = jnp.pad(q5, ((0, 0), (0, DEC_ROWS - t), (0, 0), (0, 0), (0, 0)))
    eye_c = jnp.eye(2, dtype=BF16)
    wq = jnp.einsum('bthcd,cC->bhctCd', q5, eye_c).reshape(b, N_HEADS, QROWS, V_DIM).astype(F32)
    pad_t = lambda a: jnp.pad(a.reshape(b, t, -1), ((0, 0), (0, DEC_ROWS - t), (0, 0)))
    ckt = jnp.transpose(cache_k, (0, 1, 3, 4, 5, 2)).reshape(
        cache_k.shape[0], cache_k.shape[1], N_HEADS, V_DIM, PAGE_SIZE)
    cv = cache_v.reshape(cache_v.shape[0], cache_v.shape[1], PAGE_SIZE * N_HEADS, V_DIM)
    o = _sample_attention(page_table, lam, wq, pad_t(kf), pad_t(vf), p['subln_g'][None, :], ckt,
                          cv, layer, t, 1.0 - lam_init)
    o = o[:, :t].reshape(n, ATT_WIDTH).astype(BF16)

    tpad = DEC_ROWS
    pad_c = lambda a: jnp.pad(a.reshape(b, t, -1), ((0, 0), (0, tpad - t), (0, 0)))
    xbc3 = xbc.reshape(b, t, CONV_DIM)
    y_ssm, new_ssm = _ssd(pad_c(xbc), pad_c(z), pad_c(dt), conv_state, ssm_state, p, t, tpad)
    y_ssm = y_ssm[:, :t].reshape(n, D_INNER)
    y = _finish(o, y_ssm, ga, gb, x2d, p, w, tm)
    new_conv = jnp.concatenate([conv_state, xbc3], axis=1)[:, -(CONV_W - 1):]
    return (y.reshape(b, t, D_MODEL), kf.reshape(b, t, N_HEADS, 2, HEAD_DIM),
            vf.reshape(b, t, N_HEADS, V_DIM), new_conv, new_ssm)


def kernel(x_prompt, x_sample, cache_k, cache_v, state_conv, state_ssm, page_table, norm_mix_g, w_in, q_norm_g, k_norm_g, lambda_q1, lambda_k1, lambda_q2, lambda_k2, subln_g, conv_w, conv_b, dt_bias, a_log, d_skip, ssm_norm_g, w_branch_a, w_branch_b, w_out, norm_mlp_g, w_up, w_down):
    depth = w_in.shape[0]
    yp, ys = x_prompt, x_sample
    outs = [[] for _ in range(8)]
    for l in range(depth):
        p = {
            'norm_mix_g': norm_mix_g[l], 'w_in': w_in[l], 'q_norm_g': q_norm_g[l], 'k_norm_g': k_norm_g[l],
            'lambda_q1': lambda_q1[l], 'lambda_k1': lambda_k1[l], 'lambda_q2': lambda_q2[l], 'lambda_k2': lambda_k2[l],
            'subln_g': subln_g[l], 'conv_w': conv_w[l], 'conv_b': conv_b[l], 'dt_bias': dt_bias[l],
            'a_log': a_log[l], 'd_skip': d_skip[l], 'ssm_norm_g': ssm_norm_g[l],
            'w_branch_a': w_branch_a[l], 'w_branch_b': w_branch_b[l], 'w_out': w_out[l],
            'norm_mlp_g': norm_mlp_g[l], 'w_up': w_up[l], 'w_down': w_down[l],
        }
        lam_init = 0.8 - 0.6 * math.exp(-0.3 * l)
        lam = _lam(p, lam_init)
        w = _pack_weights(p)
        yp, kp, vp, cp, sp = _prompt_layer(yp, p, w, lam, lam_init)
        ys, ks, vs, cs, ss = _sample_layer(ys, p, w, lam, lam_init, cache_k, cache_v, page_table, l,
                                           state_conv[l], state_ssm[l])
        for lst, val in zip(outs, (kp, vp, cp, sp, ks, vs, cs, ss)):
            lst.append(val)
    return (yp, ys) + tuple(jnp.stack(o) for o in outs)
```
